```python
import functools
import jax, jax.numpy as jnp
from jax import lax
import numpy as np

D_MODEL = 4096
BATCH = 8
SEQ = 2048
DEPTH = 4
DEC_BATCH = 8
DEC_SEQ = 64
PAST_LEN = 2048

CHUNK = 64
HEAD_DIM = 128
ATT_HEADS = 16
ATT_KV_HEADS = 4
ATT_GROUP = ATT_HEADS // ATT_KV_HEADS
ATT_WIDTH = ATT_HEADS * HEAD_DIM
KV_WIDTH = ATT_KV_HEADS * HEAD_DIM
WINDOW = 128
WINDOW_CHUNKS = WINDOW // CHUNK
BAND = WINDOW + CHUNK
RET_HEADS = 16
RET_DK = 128
RET_DV = 256
RET_QK_WIDTH = RET_HEADS * RET_DK
RET_V_WIDTH = RET_HEADS * RET_DV
FFN_DIM = ((8 * D_MODEL // 3 + 255) // 256) * 256
IN_SPLITS = (ATT_WIDTH, KV_WIDTH, KV_WIDTH, RET_QK_WIDTH, RET_QK_WIDTH, RET_V_WIDTH, RET_V_WIDTH, D_MODEL, D_MODEL)
IN_COLS = sum(IN_SPLITS)
ROPE_THETA = 10000.0
RMS_EPS = 1e-6
GN_EPS = 1e-5
NEG_INF = -1e30

kernel_name = 'hybrid_swa_sink_retention_stream_step'


def rmsnorm(x, g):
    xf = x.astype(jnp.float32)
    y = xf * lax.rsqrt(jnp.mean(xf * xf, axis=-1, keepdims=True) + RMS_EPS)
    return (y * g.astype(jnp.float32)).astype(x.dtype)


def rope(x, pos):
    half = x.shape[-1] // 2
    inv = ROPE_THETA ** (-jnp.arange(half, dtype=jnp.float32) / half)
    ang = pos.astype(jnp.float32)[:, None] * inv[None, :]
    cos = jnp.cos(ang)[None, :, None, :]
    sin = jnp.sin(ang)[None, :, None, :]
    x1 = x[..., :half].astype(jnp.float32)
    x2 = x[..., half:].astype(jnp.float32)
    return jnp.concatenate([x1 * cos - x2 * sin, x2 * cos + x1 * sin], axis=-1).astype(x.dtype)


def project_in(h, w_in):
    B, T, _ = h.shape
    z = jnp.einsum('btd,dc->btc', h, w_in)
    offs = [int(o) for o in np.cumsum(IN_SPLITS)[:-1]]
    q_a, k_a, v_a, q_r, k_r, v_r, g_r, gate_a, gate_r = jnp.split(z, offs, axis=-1)
    return (q_a.reshape(B, T, ATT_HEADS, HEAD_DIM),
            k_a.reshape(B, T, ATT_KV_HEADS, HEAD_DIM),
            v_a.reshape(B, T, ATT_KV_HEADS, HEAD_DIM),
            q_r.reshape(B, T, RET_HEADS, RET_DK),
            k_r.reshape(B, T, RET_HEADS, RET_DK),
            v_r.reshape(B, T, RET_HEADS, RET_DV),
            g_r, gate_a, gate_r)


def sink_softmax(s, sink):
    sink = jnp.broadcast_to(sink.astype(jnp.float32), s.shape[:-1] + (1,))
    return jax.nn.softmax(jnp.concatenate([s, sink], axis=-1), axis=-1)[..., :-1]


def swa_prompt(q, k, v, sinks):
    B, T = q.shape[:2]
    n_chunks = T // CHUNK
    qb = q.reshape(B, n_chunks, CHUNK, ATT_KV_HEADS, ATT_GROUP, HEAD_DIM) * HEAD_DIM ** -0.5
    pad = ((0, 0), (WINDOW, 0), (0, 0), (0, 0))
    kp = jnp.pad(k, pad).reshape(B, n_chunks + WINDOW_CHUNKS, CHUNK, ATT_KV_HEADS, HEAD_DIM)
    vp = jnp.pad(v, pad).reshape(B, n_chunks + WINDOW_CHUNKS, CHUNK, ATT_KV_HEADS, HEAD_DIM)
    k_band = jnp.concatenate([kp[:, i:i + n_chunks] for i in range(WINDOW_CHUNKS + 1)], axis=2)
    v_band = jnp.concatenate([vp[:, i:i + n_chunks] for i in range(WINDOW_CHUNKS + 1)], axis=2)
    s = jnp.einsum('bcqkgd,bcskd->bckgqs', qb, k_band).astype(jnp.float32)
    key_pos = jnp.arange(n_chunks)[:, None] * CHUNK - WINDOW + jnp.arange(BAND)[None, :]
    s = jnp.where((key_pos >= 0)[None, :, None, None, None, :], s, NEG_INF)
    p = sink_softmax(s, sinks.reshape(ATT_KV_HEADS, ATT_GROUP)[None, None, :, :, None, None]).astype(v.dtype)
    o = jnp.einsum('bckgqs,bcskd->bcqkgd', p, v_band)
    return o.reshape(B, T, ATT_WIDTH)


def attend_prompt(q, k, v, sinks):
    return swa_prompt(q, k, v, sinks), k[:, -WINDOW:], v[:, -WINDOW:]


def attend_sample(q, k, v, sinks, cache_k, cache_v):
    B, L = q.shape[:2]
    qg = q.reshape(B, L, ATT_KV_HEADS, ATT_GROUP, HEAD_DIM) * HEAD_DIM ** -0.5
    k_all = jnp.concatenate([cache_k, k], axis=1)
    v_all = jnp.concatenate([cache_v, v], axis=1)
    s = jnp.einsum('bqkgd,bskd->bkgqs', qg, k_all).astype(jnp.float32)
    p = sink_softmax(s, sinks.reshape(ATT_KV_HEADS, ATT_GROUP)[None, :, :, None, None]).astype(v.dtype)
    o = jnp.einsum('bkgqs,bskd->bqkgd', p, v_all).reshape(B, L, ATT_WIDTH)
    return o, k, v


def retention_log_decay():
    return jnp.log1p(-jnp.exp2(-5.0 - jnp.arange(RET_HEADS, dtype=jnp.float32)))


def retention_chunk(q, k, v, state):
    L = q.shape[1]
    lg = retention_log_decay()
    j = jnp.arange(L, dtype=jnp.float32)
    rel = j[:, None] - j[None, :]
    decay = jnp.where(rel >= 0, jnp.exp(jnp.maximum(rel, 0.0)[None] * lg[:, None, None]), 0.0).astype(q.dtype)
    xi = jnp.exp((j + 1.0)[:, None] * lg[None, :]).astype(q.dtype)
    zeta = jnp.exp((L - 1.0 - j)[:, None] * lg[None, :]).astype(q.dtype)
    chunk_decay = jnp.exp(L * lg).astype(q.dtype)
    scores = jnp.einsum('bqhd,bshd->bhqs', q, k) * decay[None]
    out = (jnp.einsum('bhqs,bshv->bqhv', scores, v)
           + jnp.einsum('bqhd,bhdv->bqhv', q, state) * xi[None, :, :, None])
    new_state = (chunk_decay[None, :, None, None] * state
                 + jnp.einsum('bshd,bshv->bhdv', k * zeta[None, :, :, None], v))
    return out, new_state


def retention_prompt(q, k, v):
    B, T = q.shape[:2]
    n_chunks = T // CHUNK

    def to_chunks(t):
        return jnp.moveaxis(t.reshape(B, n_chunks, CHUNK, *t.shape[2:]), 1, 0)

    def step(s, qkv):
        out, s_new = retention_chunk(qkv[0], qkv[1], qkv[2], s)
        return s_new, out

    s0 = jnp.zeros((B, RET_HEADS, RET_DK, RET_DV), q.dtype)
    s_final, outs = lax.scan(step, s0, (to_chunks(q), to_chunks(k), to_chunks(v)))
    return jnp.moveaxis(outs, 0, 1).reshape(B, T, RET_HEADS, RET_DV), s_final


def retention_readout(o, g, gain, w_proj):
    B, T = o.shape[:2]
    of = o.astype(jnp.float32)
    mu = jnp.mean(of, axis=-1, keepdims=True)
    var = jnp.mean(jnp.square(of - mu), axis=-1, keepdims=True)
    on = ((of - mu) * lax.rsqrt(var + GN_EPS) * gain.astype(jnp.float32)).astype(o.dtype)
    return jnp.einsum('btc,cd->btd', jax.nn.silu(g) * on.reshape(B, T, RET_V_WIDTH), w_proj)


def trunk_layer(x, pos, w_in, w_proj_a, w_proj_r, w_out, ret_gain, g_mix, g_ffn, w_gate_up, w_down, attend, retain):
    h = rmsnorm(x, g_mix)
    q_a, k_a, v_a, q_r, k_r, v_r, g_r, gate_a, gate_r = project_in(h, w_in)
    o_a, k_state, v_state = attend(rope(q_a, pos), rope(k_a, pos), v_a)
    o_r, ret_state = retain(rope(q_r, pos), rope(k_r, pos) * RET_DK ** -0.5, v_r)
    branch_a = jnp.einsum('btc,cd->btd', o_a, w_proj_a)
    branch_r = retention_readout(o_r, g_r, ret_gain, w_proj_r)
    merged = jax.nn.sigmoid(gate_a) * branch_a + jax.nn.sigmoid(gate_r) * branch_r
    x = x + jnp.einsum('btd,de->bte', merged, w_out)
    h = rmsnorm(x, g_ffn)
    gate, up = jnp.split(jnp.einsum('btd,df->btf', h, w_gate_up), 2, axis=-1)
    x = x + jnp.einsum('btf,fd->btd', jax.nn.silu(gate) * up, w_down)
    return x, k_state, v_state, ret_state


def setup_inputs(seed: int = 0) -> dict:
    key = jax.random.key(seed)
    ks = jax.random.split(key, 17)
    f32 = jnp.float32

    def normal(k, shape, scale):
        return jax.random.normal(k, shape, f32) * scale

    return {
        'x_prompt': normal(ks[0], (BATCH, SEQ, D_MODEL), 1.0),
        'x_sample': normal(ks[1], (DEC_BATCH, DEC_SEQ, D_MODEL), 1.0),
        'cache_swa_k': normal(ks[2], (DEPTH, DEC_BATCH, WINDOW, ATT_KV_HEADS, HEAD_DIM), 1.0),
        'cache_swa_v': normal(ks[3], (DEPTH, DEC_BATCH, WINDOW, ATT_KV_HEADS, HEAD_DIM), 1.0),
        'state_retention': normal(ks[4], (DEPTH, DEC_BATCH, RET_HEADS, RET_DK, RET_DV), 0.5),
        'w_in': normal(ks[5], (DEPTH, D_MODEL, IN_COLS), D_MODEL ** -0.5),
        'w_proj_a': normal(ks[6], (DEPTH, ATT_WIDTH, D_MODEL), ATT_WIDTH ** -0.5),
        'w_proj_r': normal(ks[7], (DEPTH, RET_V_WIDTH, D_MODEL), RET_V_WIDTH ** -0.5),
        'w_out': normal(ks[8], (DEPTH, D_MODEL, D_MODEL), D_MODEL ** -0.5),
        'attn_sinks': normal(ks[9], (DEPTH, ATT_HEADS), 1.0),
        'ret_norm_gain': 1.0 + normal(ks[10], (DEPTH, RET_HEADS, RET_DV), 0.02),
        'norm_mix': 1.0 + normal(ks[11], (DEPTH, D_MODEL), 0.02),
        'norm_ffn': 1.0 + normal(ks[12], (DEPTH, D_MODEL), 0.02),
        'w_gate_up': normal(ks[13], (DEPTH, D_MODEL, 2 * FFN_DIM), D_MODEL ** -0.5),
        'w_down': normal(ks[14], (DEPTH, FFN_DIM, D_MODEL), FFN_DIM ** -0.5),
        'norm_final': 1.0 + normal(ks[15], (D_MODEL,), 0.02),
    }


def reference(x_prompt, x_sample, cache_swa_k, cache_swa_v, state_retention, w_in, w_proj_a, w_proj_r, w_out,
              attn_sinks, ret_norm_gain, norm_mix, norm_ffn, w_gate_up, w_down, norm_final):
    pos_p = jnp.arange(x_prompt.shape[1], dtype=jnp.int32)
    pos_s = PAST_LEN + jnp.arange(x_sample.shape[1], dtype=jnp.int32)
    xp, xs = x_prompt, x_sample
    kp_rows, vp_rows, sp_states, ks_rows, vs_rows, ss_states = [], [], [], [], [], []
    for layer in range(DEPTH):
        weights = (w_in[layer], w_proj_a[layer], w_proj_r[layer], w_out[layer], ret_norm_gain[layer],
                   norm_mix[layer], norm_ffn[layer], w_gate_up[layer], w_down[layer])
        xp, kp, vp, sp = trunk_layer(
            xp, pos_p, *weights,
            attend=functools.partial(attend_prompt, sinks=attn_sinks[layer]),
            retain=retention_prompt)
        xs, kss, vss, sss = trunk_layer(
            xs, pos_s, *weights,
            attend=functools.partial(attend_sample, sinks=attn_sinks[layer],
                                     cache_k=cache_swa_k[layer], cache_v=cache_swa_v[layer]),
            retain=functools.partial(retention_chunk, state=state_retention[layer]))
        kp_rows.append(kp)
        vp_rows.append(vp)
        sp_states.append(sp)
        ks_rows.append(kss)
        vs_rows.append(vss)
        ss_states.append(sss)
    y_prompt = rmsnorm(xp, norm_final)
    y_sample = rmsnorm(xs, norm_final)
    new_swa_k_prompt = jnp.stack(kp_rows)
    new_swa_v_prompt = jnp.stack(vp_rows)
    new_ret_prompt = jnp.stack(sp_states)
    new_swa_k_sample = jnp.stack(ks_rows)
    new_swa_v_sample = jnp.stack(vs_rows)
    new_ret_sample = jnp.stack(ss_states)
    return (y_prompt, y_sample, new_swa_k_prompt, new_swa_v_prompt, new_ret_prompt, new_swa_k_sample, new_swa_v_sample, new_ret_sample)
```

```python
import functools

import jax
import jax.numpy as jnp
from jax import lax
from jax.experimental import pallas as pl
from jax.experimental.pallas import tpu as pltpu

F32 = jnp.float32
BF16 = jnp.bfloat16

CHUNK = 64
PAST_LEN = 2048
ROPE_THETA = 10000.0
RMS_EPS = 1e-6
GN_EPS = 1e-5

LANES = 128
VMEM_BYTES_V7X = 64 * 1024 * 1024
VMEM_BUDGET = VMEM_BYTES_V7X - 8 * 1024 * 1024


def _pick(total, target, mult):
    best = None
    for d in range(mult, min(total, target) + 1, mult):
        if total % d == 0:
            best = d
    if best is None:
        raise ValueError(f"no tile for {total} (target {target}, multiple of {mult})")
    return best


def _params(semantics, vmem_need):
    limit = min(max(int(vmem_need * 1.2), 16 * 1024 * 1024), VMEM_BUDGET)
    return pltpu.CompilerParams(dimension_semantics=semantics, vmem_limit_bytes=limit)


def _rmsnorm_kernel(x_ref, g_ref, o_ref):
    x = x_ref[...]
    y = x * lax.rsqrt(jnp.mean(x * x, axis=-1, keepdims=True) + RMS_EPS)
    o_ref[...] = (y * g_ref[...]).astype(o_ref.dtype)


def rmsnorm(x, g, out_dtype):
    rows, d = x.shape
    tr = _pick(rows, 256, 8)
    need = 2 * tr * d * (4 + jnp.dtype(out_dtype).itemsize) + 3 * tr * d * 4
    return pl.pallas_call(
        _rmsnorm_kernel,
        out_shape=jax.ShapeDtypeStruct((rows, d), out_dtype),
        grid=(rows // tr,),
        in_specs=[pl.BlockSpec((tr, d), lambda i: (i, 0)),
                  pl.BlockSpec((1, d), lambda i: (0, 0))],
        out_specs=pl.BlockSpec((tr, d), lambda i: (i, 0)),
        compiler_params=_params(("parallel",), need),
        name="rmsnorm",
    )(x, g.reshape(1, d))


KIND_PLAIN, KIND_ROPE, KIND_ROPE_SCALED = 0, 1, 2


def _inproj_kernel(kind_ref, h_ref, w_ref, cos_ref, sin_ref, o_ref, *, scale):
    kind = kind_ref[pl.program_id(1)]
    acc = jnp.dot(h_ref[...], w_ref[...], preferred_element_type=F32)

    @pl.when(kind == KIND_PLAIN)
    def _():
        o_ref[...] = acc

    @pl.when(kind != KIND_PLAIN)
    def _():
        s = jnp.where(kind == KIND_ROPE_SCALED, F32(scale), F32(1.0))
        c = cos_ref[...]
        sn = sin_ref[...]
        for g in range(acc.shape[1] // LANES):
            xg = acc[:, g * LANES:(g + 1) * LANES]
            r = xg * c + pltpu.roll(xg, LANES // 2, 1) * sn
            o_ref[:, g * LANES:(g + 1) * LANES] = r * s


def inproj(h, w, kinds, cos_t, sin_t, tn, scale):
    rows, d = h.shape
    n = w.shape[1]
    tm = _pick(rows, 768, 8)
    need = 2 * (tm * d * 2 + d * tn * 2 + tm * tn * 4 + 2 * tm * LANES * 4) + 3 * tm * tn * 4
    grid_spec = pltpu.PrefetchScalarGridSpec(
        num_scalar_prefetch=1,
        grid=(rows // tm, n // tn),
        in_specs=[pl.BlockSpec((tm, d), lambda i, j, kr: (i, 0)),
                  pl.BlockSpec((d, tn), lambda i, j, kr: (0, j)),
                  pl.BlockSpec((tm, LANES), lambda i, j, kr: (i, 0)),
                  pl.BlockSpec((tm, LANES), lambda i, j, kr: (i, 0))],
        out_specs=pl.BlockSpec((tm, tn), lambda i, j, kr: (i, j)),
    )
    return pl.pallas_call(
        functools.partial(_inproj_kernel, scale=scale),
        out_shape=jax.ShapeDtypeStruct((rows, n), F32),
        grid_spec=grid_spec,
        compiler_params=_params(("parallel", "arbitrary"), need),
        name="inproj_rope",
    )(kinds, h, w, cos_t, sin_t)


def _sink_column(sink_ref, kvh, group, rows_per_head):
    cols = [jnp.full((rows_per_head, 1), sink_ref[kvh * group + g], F32) for g in range(group)]
    return jnp.concatenate(cols, axis=0)


def _swa_chunk(q_rows, kb, vb, sink_col, group):
    hd = kb.shape[1]
    q = jnp.concatenate([q_rows[:, g * hd:(g + 1) * hd] for g in range(group)], axis=0).astype(BF16)
    s = lax.dot_general(q, kb, (((1,), (1,)), ((), ())), preferred_element_type=F32)
    m = jnp.maximum(jnp.max(s, axis=-1, keepdims=True), sink_col)
    p = jnp.exp(s - m)
    denom = jnp.sum(p, axis=-1, keepdims=True) + jnp.exp(sink_col - m)
    p = (p / denom).astype(BF16)
    return jnp.dot(p, vb, preferred_element_type=F32)


def _swa_prompt_kernel(sink_ref, q_ref, k_ref, v_ref, o_ref, kb_ref, vb_ref, *, group, window_chunks):
    seq, hd = k_ref.shape
    n_chunks = seq // CHUNK
    kb_ref[...] = k_ref[...].astype(BF16)
    vb_ref[...] = v_ref[...].astype(BF16)
    sink_col = _sink_column(sink_ref, pl.program_id(1), group, CHUNK)

    def do_chunk(c, start, n):
        row0 = c * CHUNK if isinstance(c, int) else pl.multiple_of(c * CHUNK, CHUNK)
        o = _swa_chunk(q_ref[pl.ds(row0, CHUNK), :], kb_ref[pl.ds(start, n), :], vb_ref[pl.ds(start, n), :],
                       sink_col, group)
        for g in range(group):
            o_ref[pl.ds(row0, CHUNK), g * hd:(g + 1) * hd] = o[g * CHUNK:(g + 1) * CHUNK].astype(o_ref.dtype)

    for c in range(min(window_chunks, n_chunks)):
        do_chunk(c, 0, (c + 1) * CHUNK)

    def body(c, carry):
        do_chunk(c, pl.multiple_of((c - window_chunks) * CHUNK, CHUNK), (window_chunks + 1) * CHUNK)
        return carry

    lax.fori_loop(window_chunks, n_chunks, body, 0)


def _swa_sample_kernel(sink_ref, q_ref, k_ref, v_ref, ck_ref, cv_ref, o_in_ref, o_ref, *, group):
    del o_in_ref
    hd = k_ref.shape[1]
    kb = jnp.concatenate([ck_ref[...], k_ref[...]], axis=0).astype(BF16)
    vb = jnp.concatenate([cv_ref[...], v_ref[...]], axis=0).astype(BF16)
    sink_col = _sink_column(sink_ref, pl.program_id(1), group, q_ref.shape[0])
    o = _swa_chunk(q_ref[...], kb, vb, sink_col, group)
    rows = q_ref.shape[0]
    for g in range(group):
        o_ref[:, g * hd:(g + 1) * hd] = o[g * rows:(g + 1) * rows].astype(o_ref.dtype)


def swa(z, sinks, cache_k, cache_v, layer, cfg):
    hd, kvh, group = cfg["head_dim"], cfg["kv_heads"], cfg["att_group"]
    b_p, t_p, b_s, t_s = cfg["b_p"], cfg["t_p"], cfg["b_s"], cfg["t_s"]
    rows = z.shape[0]
    qw = group * hd
    k_blk0 = cfg["off_k_a"] // hd
    v_blk0 = cfg["off_v_a"] // hd
    att_w = cfg["att_width"]
    window = cache_k.shape[2]
    smem = pl.BlockSpec(memory_space=pltpu.SMEM)

    need_p = 2 * (t_p * qw * 4 + 2 * t_p * hd * 4 + t_p * qw * 2) + 2 * t_p * hd * 2 + (4 << 20)
    o = pl.pallas_call(
        functools.partial(_swa_prompt_kernel, group=group, window_chunks=window // CHUNK),
        out_shape=jax.ShapeDtypeStruct((rows, att_w), BF16),
        grid=(b_p, kvh),
        in_specs=[smem,
                  pl.BlockSpec((t_p, qw), lambda b, k: (b, k)),
                  pl.BlockSpec((t_p, hd), lambda b, k: (b, k_blk0 + k)),
                  pl.BlockSpec((t_p, hd), lambda b, k: (b, v_blk0 + k))],
        out_specs=pl.BlockSpec((t_p, qw), lambda b, k: (b, k)),
        scratch_shapes=[pltpu.VMEM((t_p, hd), BF16), pltpu.VMEM((t_p, hd), BF16)],
        compiler_params=_params(("parallel", "parallel"), need_p),
        name="swa_prompt",
    )(sinks, z, z, z)

    assert t_s == CHUNK and window + t_s == (window // CHUNK + 1) * CHUNK
    rb0 = (b_p * t_p) // t_s
    o = pl.pallas_call(
        functools.partial(_swa_sample_kernel, group=group),
        out_shape=jax.ShapeDtypeStruct((rows, att_w), BF16),
        grid=(b_s, kvh),
        in_specs=[smem,
                  pl.BlockSpec((t_s, qw), lambda b, k: (rb0 + b, k)),
                  pl.BlockSpec((t_s, hd), lambda b, k: (rb0 + b, k_blk0 + k)),
                  pl.BlockSpec((t_s, hd), lambda b, k: (rb0 + b, v_blk0 + k)),
                  pl.BlockSpec((None, None, window, hd), lambda b, k: (layer, b, 0, k)),
                  pl.BlockSpec((None, None, window, hd), lambda b, k: (layer, b, 0, k)),
                  pl.BlockSpec(memory_space=pl.ANY)],
        out_specs=pl.BlockSpec((t_s, qw), lambda b, k: (rb0 + b, k)),
        input_output_aliases={6: 0},
        compiler_params=_params(("parallel", "parallel"), 8 << 20),
        name="swa_sample",
    )(sinks, z, z, z, cache_k, cache_v, o)
    return o


def _retention_chunk(q32, k32, v32, g32, state, decay, xi, zeta, chunk_decay, gain):
    q = q32.astype(BF16)
    v = v32.astype(BF16)
    s = lax.dot_general(q, k32.astype(BF16), (((1,), (1,)), ((), ())), preferred_element_type=F32) * decay
    o = (jnp.dot(s.astype(BF16), v, preferred_element_type=F32)
         + jnp.dot(q, state.astype(BF16), preferred_element_type=F32) * xi)
    kz_t = jnp.transpose(k32 * zeta).astype(BF16)
    new_state = chunk_decay * state + jnp.dot(kz_t, v, preferred_element_type=F32)
    mu = jnp.mean(o, axis=-1, keepdims=True)
    var = jnp.mean(jnp.square(o - mu), axis=-1, keepdims=True)
    on = (o - mu) * lax.rsqrt(var + GN_EPS) * gain
    return (g32 * jax.nn.sigmoid(g32)) * on, new_state


def _retention_prompt_kernel(cd_ref, q_ref, k_ref, v_ref, g_ref, decay_ref, xi_ref, zeta_ref, gain_ref,
                             o_ref, st_ref):
    h = pl.program_id(1)
    n_chunks = q_ref.shape[0] // CHUNK
    decay, xi, zeta = decay_ref[...], xi_ref[...], zeta_ref[...]
    gain = gain_ref[pl.ds(h, 1), :]
    chunk_decay = cd_ref[h]

    def body(c, state):
        rows = pl.ds(pl.multiple_of(c * CHUNK, CHUNK), CHUNK)
        out, new_state = _retention_chunk(q_ref[rows, :], k_ref[rows, :], v_ref[rows, :], g_ref[rows, :],
                                          state, decay, xi, zeta, chunk_decay, gain)
        o_ref[rows, :] = out.astype(o_ref.dtype)
        return new_state

    st_ref[...] = lax.fori_loop(0, n_chunks, body, jnp.zeros(st_ref.shape, F32))


def _retention_sample_kernel(cd_ref, q_ref, k_ref, v_ref, g_ref, decay_ref, xi_ref, zeta_ref, gain_ref,
                             st_in_ref, o_in_ref, o_ref, st_ref):
    del o_in_ref
    h = pl.program_id(1)
    out, new_state = _retention_chunk(q_ref[...], k_ref[...], v_ref[...], g_ref[...], st_in_ref[...],
                                      decay_ref[...], xi_ref[...], zeta_ref[...], cd_ref[h],
                                      gain_ref[pl.ds(h, 1), :])
    o_ref[...] = out.astype(o_ref.dtype)
    st_ref[...] = new_state


def retention(z, state_in, gain, layer, consts, cfg):
    dk, dv, heads = cfg["ret_dk"], cfg["ret_dv"], cfg["ret_heads"]
    b_p, t_p, b_s, t_s = cfg["b_p"], cfg["t_p"], cfg["b_s"], cfg["t_s"]
    rows = z.shape[0]
    q0, k0 = cfg["off_q_r"] // dk, cfg["off_k_r"] // dk
    v0, g0 = cfg["off_v_r"] // dv, cfg["off_g_r"] // dv
    decay, xi, zeta, chunk_decay = consts
    smem = pl.BlockSpec(memory_space=pltpu.SMEM)
    const_specs = [pl.BlockSpec((None, CHUNK, CHUNK), lambda b, h: (h, 0, 0)),
                   pl.BlockSpec((None, CHUNK, 1), lambda b, h: (h, 0, 0)),
                   pl.BlockSpec((None, CHUNK, 1), lambda b, h: (h, 0, 0)),
                   pl.BlockSpec((heads, dv), lambda b, h: (0, 0))]

    need_p = 2 * (2 * t_p * dk * 4 + 2 * t_p * dv * 4 + t_p * dv * 2) + (6 << 20)
    rr, st_p = pl.pallas_call(
        _retention_prompt_kernel,
        out_shape=(jax.ShapeDtypeStruct((rows, heads * dv), BF16),
                   jax.ShapeDtypeStruct((b_p, heads, dk, dv), F32)),
        grid=(b_p, heads),
        in_specs=[smem,
                  pl.BlockSpec((t_p, dk), lambda b, h: (b, q0 + h)),
                  pl.BlockSpec((t_p, dk), lambda b, h: (b, k0 + h)),
                  pl.BlockSpec((t_p, dv), lambda b, h: (b, v0 + h)),
                  pl.BlockSpec((t_p, dv), lambda b, h: (b, g0 + h))] + const_specs,
        out_specs=(pl.BlockSpec((t_p, dv), lambda b, h: (b, h)),
                   pl.BlockSpec((None, None, dk, dv), lambda b, h: (b, h, 0, 0))),
        compiler_params=_params(("parallel", "parallel"), need_p),
        name="retention_prompt",
    )(chunk_decay, z, z, z, z, decay, xi, zeta, gain)

    assert t_s == CHUNK
    rb0 = (b_p * t_p) // t_s
    rr, st_s = pl.pallas_call(
        _retention_sample_kernel,
        out_shape=(jax.ShapeDtypeStruct((rows, heads * dv), BF16),
                   jax.ShapeDtypeStruct((b_s, heads, dk, dv), F32)),
        grid=(b_s, heads),
        in_specs=[smem,
                  pl.BlockSpec((t_s, dk), lambda b, h: (rb0 + b, q0 + h)),
                  pl.BlockSpec((t_s, dk), lambda b, h: (rb0 + b, k0 + h)),
                  pl.BlockSpec((t_s, dv), lambda b, h: (rb0 + b, v0 + h)),
                  pl.BlockSpec((t_s, dv), lambda b, h: (rb0 + b, g0 + h))] + const_specs +
                 [pl.BlockSpec((None, None, None, dk, dv), lambda b, h: (layer, b, h, 0, 0)),
                  pl.BlockSpec(memory_space=pl.ANY)],
        out_specs=(pl.BlockSpec((t_s, dv), lambda b, h: (rb0 + b, h)),
                   pl.BlockSpec((None, None, dk, dv), lambda b, h: (b, h, 0, 0))),
        input_output_aliases={10: 0},
        compiler_params=_params(("parallel", "parallel"), 8 << 20),
        name="retention_sample",
    )(chunk_decay, z, z, z, z, decay, xi, zeta, gain, state_in, rr)
    return rr, st_p, st_s


def _merge_kernel(oa_ref, rr_ref, wa_ref, wr_ref, ga_ref, gr_ref, o_ref):
    branch_a = jnp.dot(oa_ref[...], wa_ref[...], preferred_element_type=F32)
    branch_r = jnp.dot(rr_ref[...], wr_ref[...], preferred_element_type=F32)
    merged = jax.nn.sigmoid(ga_ref[...]) * branch_a + jax.nn.sigmoid(gr_ref[...]) * branch_r
    o_ref[...] = merged.astype(o_ref.dtype)


def merge_branches(o_a, rr, w_a, w_r, z, cfg):
    rows, ka = o_a.shape
    kr = rr.shape[1]
    d = w_a.shape[1]
    tm = _pick(rows, 768, 8)
    tn = _pick(d, 512, LANES)
    ga0, gr0 = cfg["off_gate_a"] // tn, cfg["off_gate_r"] // tn
    assert cfg["off_gate_a"] % tn == 0 and cfg["off_gate_r"] % tn == 0
    need = 2 * (tm * (ka + kr) * 2 + (ka + kr) * tn * 2 + 2 * tm * tn * 4 + tm * tn * 2) + 4 * tm * tn * 4
    return pl.pallas_call(
        _merge_kernel,
        out_shape=jax.ShapeDtypeStruct((rows, d), BF16),
        grid=(rows // tm, d // tn),
        in_specs=[pl.BlockSpec((tm, ka), lambda i, j: (i, 0)),
                  pl.BlockSpec((tm, kr), lambda i, j: (i, 0)),
                  pl.BlockSpec((ka, tn), lambda i, j: (0, j)),
                  pl.BlockSpec((kr, tn), lambda i, j: (0, j)),
                  pl.BlockSpec((tm, tn), lambda i, j: (i, ga0 + j)),
                  pl.BlockSpec((tm, tn), lambda i, j: (i, gr0 + j))],
        out_specs=pl.BlockSpec((tm, tn), lambda i, j: (i, j)),
        compiler_params=_params(("parallel", "arbitrary"), need),
        name="merge_branches",
    )(o_a, rr, w_a, w_r, z, z)


def _matmul_residual_kernel(a_ref, w_ref, x_ref, o_ref):
    o_ref[...] = x_ref[...] + jnp.dot(a_ref[...], w_ref[...], preferred_element_type=F32)


def matmul_residual(a, w, x, tm_target, tn_target):
    rows, k = a.shape
    n = w.shape[1]
    tm = _pick(rows, tm_target, 8)
    tn = _pick(n, tn_target, LANES)
    need = 2 * (tm * k * 2 + k * tn * 2 + 2 * tm * tn * 4) + 2 * tm * tn * 4
    return pl.pallas_call(
        _matmul_residual_kernel,
        out_shape=jax.ShapeDtypeStruct((rows, n), F32),
        grid=(rows // tm, n // tn),
        in_specs=[pl.BlockSpec((tm, k), lambda i, j: (i, 0)),
                  pl.BlockSpec((k, tn), lambda i, j: (0, j)),
                  pl.BlockSpec((tm, tn), lambda i, j: (i, j))],
        out_specs=pl.BlockSpec((tm, tn), lambda i, j: (i, j)),
        compiler_params=_params(("parallel", "arbitrary"), need),
        name="matmul_residual",
    )(a, w, x)


def _gate_up_kernel(h_ref, wg_ref, wu_ref, o_ref):
    h = h_ref[...]
    gate = jnp.dot(h, wg_ref[...], preferred_element_type=F32)
    up = jnp.dot(h, wu_ref[...], preferred_element_type=F32)
    o_ref[...] = ((gate * jax.nn.sigmoid(gate)) * up).astype(o_ref.dtype)


def gate_up(h, w_gate_up):
    rows, d = h.shape
    f = w_gate_up.shape[1] // 2
    tm = _pick(rows, 768, 8)
    tn = _pick(f, 256, LANES)
    nj = f // tn
    need = 2 * (tm * d * 2 + 2 * d * tn * 2 + tm * tn * 2) + 4 * tm * tn * 4
    return pl.pallas_call(
        _gate_up_kernel,
        out_shape=jax.ShapeDtypeStruct((rows, f), BF16),
        grid=(rows // tm, nj),
        in_specs=[pl.BlockSpec((tm, d), lambda i, j: (i, 0)),
                  pl.BlockSpec((d, tn), lambda i, j: (0, j)),
                  pl.BlockSpec((d, tn), lambda i, j: (0, nj + j))],
        out_specs=pl.BlockSpec((tm, tn), lambda i, j: (i, j)),
        compiler_params=_params(("parallel", "arbitrary"), need),
        name="swiglu_gate_up",
    )(h, w_gate_up, w_gate_up)


def _rope_tables(positions, hd):
    half = hd // 2
    inv = ROPE_THETA ** (-jnp.arange(half, dtype=F32) / half)
    ang = positions.astype(F32)[:, None] * inv[None, :]
    cos, sin = jnp.cos(ang), jnp.sin(ang)
    return jnp.concatenate([cos, cos], axis=-1), jnp.concatenate([-sin, sin], axis=-1)


def _retention_tables(heads, length):
    lg = jnp.log1p(-jnp.exp2(-5.0 - jnp.arange(heads, dtype=F32)))
    j = jnp.arange(length, dtype=F32)
    rel = j[:, None] - j[None, :]
    decay = jnp.where(rel >= 0, jnp.exp(jnp.maximum(rel, 0.0)[None] * lg[:, None, None]), 0.0)
    xi = jnp.exp((j + 1.0)[None, :] * lg[:, None])[:, :, None]
    zeta = jnp.exp((length - 1.0 - j)[None, :] * lg[:, None])[:, :, None]
    chunk_decay = jnp.exp(length * lg)
    return decay, xi, zeta, chunk_decay


def kernel(x_prompt, x_sample, cache_swa_k, cache_swa_v, state_retention, w_in, w_proj_a, w_proj_r, w_out,
           attn_sinks, ret_norm_gain, norm_mix, norm_ffn, w_gate_up, w_down, norm_final):
    b_p, t_p, d = x_prompt.shape
    b_s, t_s, _ = x_sample.shape
    depth = w_in.shape[0]
    _, _, window, kvh, hd = cache_swa_k.shape
    att_heads = attn_sinks.shape[1]
    _, _, ret_heads, ret_dk, ret_dv = state_retention.shape
    att_w, kv_w = att_heads * hd, kvh * hd
    rqk_w, rv_w = ret_heads * ret_dk, ret_heads * ret_dv
    splits = (att_w, kv_w, kv_w, rqk_w, rqk_w, rv_w, rv_w, d, d)
    offs = [0]
    for s in splits:
        offs.append(offs[-1] + s)
    assert offs[-1] == w_in.shape[2] and hd == LANES and ret_dk == LANES
    cfg = dict(b_p=b_p, t_p=t_p, b_s=b_s, t_s=t_s, head_dim=hd, kv_heads=kvh, att_group=att_heads // kvh,
               att_width=att_w, ret_heads=ret_heads, ret_dk=ret_dk, ret_dv=ret_dv,
               off_k_a=offs[1], off_v_a=offs[2], off_q_r=offs[3], off_k_r=offs[4], off_v_r=offs[5],
               off_g_r=offs[6], off_gate_a=offs[7], off_gate_r=offs[8])

    tn_in = _pick(kv_w, 512, LANES)
    assert all(o % tn_in == 0 for o in offs)
    kind_of_split = (KIND_ROPE_SCALED, KIND_ROPE, KIND_PLAIN, KIND_ROPE, KIND_ROPE_SCALED,
                     KIND_PLAIN, KIND_PLAIN, KIND_PLAIN, KIND_PLAIN)
    assert hd == ret_dk
    kinds = jnp.asarray([kd for kd, s in zip(kind_of_split, splits) for _ in range(s // tn_in)], jnp.int32)

    pos = jnp.concatenate([jnp.tile(jnp.arange(t_p, dtype=jnp.int32), b_p),
                           jnp.tile(PAST_LEN + jnp.arange(t_s, dtype=jnp.int32), b_s)])
    cos_t, sin_t = _rope_tables(pos, hd)
    ret_consts = _retention_tables(ret_heads, CHUNK)

    cache_k = cache_swa_k.reshape(depth, b_s, window, kv_w)
    cache_v = cache_swa_v.reshape(depth, b_s, window, kv_w)

    rows_p = b_p * t_p
    x = jnp.concatenate([x_prompt.reshape(rows_p, d), x_sample.reshape(b_s * t_s, d)], axis=0)

    kp_rows, vp_rows, sp_states, ks_rows, vs_rows, ss_states = [], [], [], [], [], []
    for layer in range(depth):
        h = rmsnorm(x, norm_mix[layer], BF16)
        z = inproj(h, w_in[layer].astype(BF16), kinds, cos_t, sin_t, tn_in, hd ** -0.5)
        o_a = swa(z, attn_sinks[layer], cache_k, cache_v, layer, cfg)
        rr, st_p, st_s = retention(z, state_retention, ret_norm_gain[layer], layer, ret_consts, cfg)
        merged = merge_branches(o_a, rr, w_proj_a[layer].astype(BF16), w_proj_r[layer].astype(BF16), z, cfg)
        x = matmul_residual(merged, w_out[layer].astype(BF16), x, 768, 512)
        h = rmsnorm(x, norm_ffn[layer], BF16)
        act = gate_up(h, w_gate_up[layer].astype(BF16))
        x = matmul_residual(act, w_down[layer].astype(BF16), x, 512, 256)

        k_new = z[:, offs[1]:offs[2]]
        v_new = z[:, offs[2]:offs[3]]
        kp_rows.append(k_new[:rows_p].reshape(b_p, t_p, kvh, hd)[:, t_p - window:])
        vp_rows.append(v_new[:rows_p].reshape(b_p, t_p, kvh, hd)[:, t_p - window:])
        ks_rows.append(k_new[rows_p:].reshape(b_s, t_s, kvh, hd))
        vs_rows.append(v_new[rows_p:].reshape(b_s, t_s, kvh, hd))
        sp_states.append(st_p)
        ss_states.append(st_s)

    y = rmsnorm(x, norm_final, F32)
    return (y[:rows_p].reshape(b_p, t_p, d), y[rows_p:].reshape(b_s, t_s, d),
            jnp.stack(kp_rows), jnp.stack(vp_rows), jnp.stack(sp_states),
            jnp.stack(ks_rows), jnp.stack(vs_rows), jnp.stack(ss_states))
```

```python
import functools
import math

import jax
import jax.numpy as jnp
from jax import lax
from jax.experimental import pallas as pl
from jax.experimental.pallas import tpu as pltpu

F32 = jnp.float32
BF16 = jnp.bfloat16

CHUNK = 64
PAST_LEN = 2048
ROPE_THETA = 10000.0
RMS_EPS = 1e-6
GN_EPS = 1e-5

LANES = 128
VMEM_BYTES_V7X = 64 * 1024 * 1024
VMEM_BUDGET = VMEM_BYTES_V7X - 8 * 1024 * 1024

RET_BLOCK_TARGET = 256


def _pick(total, target, mult):
    best = None
    for d in range(mult, min(total, target) + 1, mult):
        if total % d == 0:
            best = d
    if best is None:
        raise ValueError(f"no tile for {total} (target {target}, multiple of {mult})")
    return best


def _params(semantics, vmem_need):
    limit = min(max(int(vmem_need * 1.2), 16 * 1024 * 1024), VMEM_BUDGET)
    return pltpu.CompilerParams(dimension_semantics=semantics, vmem_limit_bytes=limit)


def _row_parts(ref, parts):
    rows = ref.shape[0]
    sub = rows // parts
    return [slice(p * sub, (p + 1) * sub) for p in range(parts)]


def _rmsnorm_kernel(x_ref, g_ref, o_ref):
    x = x_ref[...]
    y = x * lax.rsqrt(jnp.mean(x * x, axis=-1, keepdims=True) + RMS_EPS)
    o_ref[...] = (y * g_ref[...]).astype(o_ref.dtype)


def rmsnorm(x, g, out_dtype):
    rows, d = x.shape
    tr = _pick(rows, 256, 8)
    need = 2 * tr * d * (4 + jnp.dtype(out_dtype).itemsize) + 3 * tr * d * 4
    return pl.pallas_call(
        _rmsnorm_kernel,
        out_shape=jax.ShapeDtypeStruct((rows, d), out_dtype),
        grid=(rows // tr,),
        in_specs=[pl.BlockSpec((tr, d), lambda i: (i, 0)),
                  pl.BlockSpec((1, d), lambda i: (0, 0))],
        out_specs=pl.BlockSpec((tr, d), lambda i: (i, 0)),
        compiler_params=_params(("parallel",), need),
        name="rmsnorm",
    )(x, g.reshape(1, d))


def _rmsnorm_split_kernel(x_ref, g_ref, head_ref, tail_ref, *, head_blocks):
    x = x_ref[...]
    y = x * lax.rsqrt(jnp.mean(x * x, axis=-1, keepdims=True) + RMS_EPS) * g_ref[...]
    i = pl.program_id(0)

    @pl.when(i < head_blocks)
    def _():
        head_ref[...] = y

    @pl.when(i >= head_blocks)
    def _():
        tail_ref[...] = y


def rmsnorm_split(x, g, head_rows):
    rows, d = x.shape
    assert 0 < head_rows < rows
    tr = _pick(math.gcd(head_rows, rows - head_rows), 256, 8)
    hb = head_rows // tr
    return pl.pallas_call(
        functools.partial(_rmsnorm_split_kernel, head_blocks=hb),
        out_shape=(jax.ShapeDtypeStruct((head_rows, d), F32), jax.ShapeDtypeStruct((rows - head_rows, d), F32)),
        grid=(rows // tr,),
        in_specs=[pl.BlockSpec((tr, d), lambda i: (i, 0)),
                  pl.BlockSpec((1, d), lambda i: (0, 0))],
        out_specs=(pl.BlockSpec((tr, d), lambda i: (jnp.minimum(i, hb - 1), 0)),
                   pl.BlockSpec((tr, d), lambda i: (jnp.maximum(i - hb, 0), 0))),
        compiler_params=_params(("arbitrary",), 10 * tr * d * 4),
        name="rmsnorm_split",
    )(x, g.reshape(1, d))


KIND_PLAIN, KIND_ROPE, KIND_ROPE_SCALED = 0, 1, 2


def _inproj_kernel(kind_ref, h_ref, w_ref, cos_ref, sin_ref, o_ref, *, scale, parts):
    kind = kind_ref[pl.program_id(1)]

    @pl.when(kind == KIND_PLAIN)
    def _():
        o_ref[...] = jnp.dot(h_ref[...], w_ref[...], preferred_element_type=F32)

    @pl.when(kind != KIND_PLAIN)
    def _():
        s = jnp.where(kind == KIND_ROPE_SCALED, F32(scale), F32(1.0))
        for rows in _row_parts(o_ref, parts):
            acc = jnp.dot(h_ref[rows, :], w_ref[...], preferred_element_type=F32)
            c = cos_ref[rows, :]
            sn = sin_ref[rows, :]
            for g in range(acc.shape[1] // LANES):
                xg = acc[:, g * LANES:(g + 1) * LANES]
                r = xg * c + pltpu.roll(xg, LANES // 2, 1) * sn
                o_ref[rows, g * LANES:(g + 1) * LANES] = r * s


def inproj(h, w, layer, kinds, cos_t, sin_t, tn, scale):
    rows, d = h.shape
    n = w.shape[2]
    tm = _pick(rows, 1536, 8)
    need = 2 * (tm * d * 2 + d * tn * 2 + tm * tn * 4 + 2 * tm * LANES * 4) + 4 * tm * LANES * 4
    grid_spec = pltpu.PrefetchScalarGridSpec(
        num_scalar_prefetch=1,
        grid=(rows // tm, n // tn),
        in_specs=[pl.BlockSpec((tm, d), lambda i, j, kr: (i, 0)),
                  pl.BlockSpec((None, d, tn), lambda i, j, kr: (layer, 0, j)),
                  pl.BlockSpec((tm, LANES), lambda i, j, kr: (i, 0)),
                  pl.BlockSpec((tm, LANES), lambda i, j, kr: (i, 0))],
        out_specs=pl.BlockSpec((tm, tn), lambda i, j, kr: (i, j)),
    )
    return pl.pallas_call(
        functools.partial(_inproj_kernel, scale=scale, parts=4 if tm % 32 == 0 else 1),
        out_shape=jax.ShapeDtypeStruct((rows, n), F32),
        grid_spec=grid_spec,
        compiler_params=_params(("parallel", "arbitrary"), need),
        name="inproj_rope",
    )(kinds, h, w, cos_t, sin_t)


def _sink_column(sink_ref, kvh, group, rows_per_head):
    cols = [jnp.full((rows_per_head, 1), sink_ref[kvh * group + g], F32) for g in range(group)]
    return jnp.concatenate(cols, axis=0)


def _swa_chunk(q_rows, kb, vb, sink_col, group):
    hd = kb.shape[1]
    q = jnp.concatenate([q_rows[:, g * hd:(g + 1) * hd] for g in range(group)], axis=0).astype(BF16)
    s = lax.dot_general(q, kb, (((1,), (1,)), ((), ())), preferred_element_type=F32)
    m = jnp.maximum(jnp.max(s, axis=-1, keepdims=True), sink_col)
    p = jnp.exp(s - m)
    denom = jnp.sum(p, axis=-1, keepdims=True) + jnp.exp(sink_col - m)
    p = (p / denom).astype(BF16)
    return jnp.dot(p, vb, preferred_element_type=F32)


def _swa_prompt_kernel(sink_ref, q_ref, k_ref, v_ref, o_ref, kc_ref, vc_ref, kb_ref, vb_ref, s_ref, p_ref, *,
                       group, window_chunks, pad_chunks, unroll):
    seq, hd = k_ref.shape
    n_chunks = seq // CHUNK
    kvh = pl.program_id(1)
    kb_ref[...] = k_ref[...].astype(BF16)
    vb_ref[...] = v_ref[...].astype(BF16)
    kc_ref[...] = k_ref[seq - kc_ref.shape[0]:, :]
    vc_ref[...] = v_ref[seq - vc_ref.shape[0]:, :]

    def store_heads(row0, o):
        for g in range(group):
            o_ref[pl.ds(row0, CHUNK), g * hd:(g + 1) * hd] = o[g * CHUNK:(g + 1) * CHUNK].astype(o_ref.dtype)

    head = min(window_chunks + pad_chunks, n_chunks)
    sink_col = _sink_column(sink_ref, kvh, group, CHUNK)
    for c in range(head):
        lo = max(c - window_chunks, 0) * CHUNK
        o = _swa_chunk(q_ref[c * CHUNK:(c + 1) * CHUNK, :], kb_ref[lo:(c + 1) * CHUNK, :],
                       vb_ref[lo:(c + 1) * CHUNK, :], sink_col, group)
        store_heads(c * CHUNK, o)

    band = (window_chunks + pad_chunks + 1) * CHUNK

    def band_rows(c):
        return pl.ds(pl.multiple_of((c - window_chunks - pad_chunks) * CHUNK, CHUNK), band)

    def scores(i, carry):
        c = head + i
        q_rows = q_ref[pl.ds(pl.multiple_of(c * CHUNK, CHUNK), CHUNK), :]
        q = jnp.concatenate([q_rows[:, g * hd:(g + 1) * hd] for g in range(group)], axis=0).astype(BF16)
        s_ref[i] = lax.dot_general(q, kb_ref[band_rows(c), :], (((1,), (1,)), ((), ())),
                                   preferred_element_type=F32)
        return carry

    outside = lax.broadcasted_iota(jnp.int32, (CHUNK, band), 1) < pad_chunks * CHUNK

    def softmax(i, carry):
        for g in range(group):
            rows = slice(g * CHUNK, (g + 1) * CHUNK)
            sink = sink_ref[kvh * group + g]
            s = jnp.where(outside, F32(-1e30), s_ref[i, rows, :])
            m = jnp.maximum(jnp.max(s, axis=-1, keepdims=True), sink)
            p = jnp.exp(s - m)
            denom = jnp.sum(p, axis=-1, keepdims=True) + jnp.exp(sink - m)
            p_ref[i, rows, :] = (p / denom).astype(BF16)
        return carry

    def values(i, carry):
        c = head + i
        o = jnp.dot(p_ref[i], vb_ref[band_rows(c), :], preferred_element_type=F32)
        store_heads(pl.multiple_of(c * CHUNK, CHUNK), o)
        return carry

    n_main = n_chunks - head
    lax.fori_loop(0, n_main, scores, 0, unroll=2 * unroll)
    lax.fori_loop(0, n_main, softmax, 0, unroll=unroll)
    lax.fori_loop(0, n_main, values, 0, unroll=2 * unroll)


def _swa_sample_kernel(sink_ref, q_ref, k_ref, v_ref, ck_ref, cv_ref, o_in_ref, o_ref, kn_ref, vn_ref, *, group):
    del o_in_ref
    hd = k_ref.shape[1]
    kn_ref[...] = k_ref[...]
    vn_ref[...] = v_ref[...]
    kb = jnp.concatenate([ck_ref[...], k_ref[...]], axis=0).astype(BF16)
    vb = jnp.concatenate([cv_ref[...], v_ref[...]], axis=0).astype(BF16)
    sink_col = _sink_column(sink_ref, pl.program_id(1), group, q_ref.shape[0])
    o = _swa_chunk(q_ref[...], kb, vb, sink_col, group)
    rows = q_ref.shape[0]
    for g in range(group):
        o_ref[:, g * hd:(g + 1) * hd] = o[g * rows:(g + 1) * rows].astype(o_ref.dtype)


def swa(z, sinks, cache_k, cache_v, layer, cfg):
    hd, kvh, group = cfg["head_dim"], cfg["kv_heads"], cfg["att_group"]
    b_p, t_p, b_s, t_s = cfg["b_p"], cfg["t_p"], cfg["b_s"], cfg["t_s"]
    rows = z.shape[0]
    qw = group * hd
    k_blk0 = cfg["off_k_a"] // hd
    v_blk0 = cfg["off_v_a"] // hd
    att_w = cfg["att_width"]
    window = cache_k.shape[2]
    window_chunks = window // CHUNK
    smem = pl.BlockSpec(memory_space=pltpu.SMEM)

    lane_chunks = LANES // CHUNK
    pad_chunks = -(window_chunks + 1) % lane_chunks
    band = (window_chunks + pad_chunks + 1) * CHUNK
    n_main = max(t_p // CHUNK - (window_chunks + pad_chunks), 0)
    need_p = (2 * (t_p * qw * 4 + 2 * t_p * hd * 4 + t_p * qw * 2) + 2 * t_p * hd * 2
              + n_main * group * CHUNK * band * 6 + (4 << 20))
    kv_w = kvh * hd
    o, kc_p, vc_p = pl.pallas_call(
        functools.partial(_swa_prompt_kernel, group=group, window_chunks=window_chunks, pad_chunks=pad_chunks,
                          unroll=2),
        out_shape=(jax.ShapeDtypeStruct((rows, att_w), BF16),
                   jax.ShapeDtypeStruct((b_p, window, kv_w), F32),
                   jax.ShapeDtypeStruct((b_p, window, kv_w), F32)),
        grid=(b_p, kvh),
        in_specs=[smem,
                  pl.BlockSpec((t_p, qw), lambda b, k: (b, k)),
                  pl.BlockSpec((t_p, hd), lambda b, k: (b, k_blk0 + k)),
                  pl.BlockSpec((t_p, hd), lambda b, k: (b, v_blk0 + k))],
        out_specs=(pl.BlockSpec((t_p, qw), lambda b, k: (b, k)),
                   pl.BlockSpec((None, window, hd), lambda b, k: (b, 0, k)),
                   pl.BlockSpec((None, window, hd), lambda b, k: (b, 0, k))),
        scratch_shapes=[pltpu.VMEM((t_p, hd), BF16), pltpu.VMEM((t_p, hd), BF16),
                        pltpu.VMEM((max(n_main, 1), group * CHUNK, band), F32),
                        pltpu.VMEM((max(n_main, 1), group * CHUNK, band), BF16)],
        compiler_params=_params(("parallel", "parallel"), need_p),
        name="swa_prompt",
    )(sinks, z, z, z)

    assert t_s == CHUNK and window + t_s == (window_chunks + 1) * CHUNK
    rb0 = (b_p * t_p) // t_s
    o, kn_s, vn_s = pl.pallas_call(
        functools.partial(_swa_sample_kernel, group=group),
        out_shape=(jax.ShapeDtypeStruct((rows, att_w), BF16),
                   jax.ShapeDtypeStruct((b_s, t_s, kv_w), F32),
                   jax.ShapeDtypeStruct((b_s, t_s, kv_w), F32)),
        grid=(b_s, kvh),
        in_specs=[smem,
                  pl.BlockSpec((t_s, qw), lambda b, k: (rb0 + b, k)),
                  pl.BlockSpec((t_s, hd), lambda b, k: (rb0 + b, k_blk0 + k)),
                  pl.BlockSpec((t_s, hd), lambda b, k: (rb0 + b, v_blk0 + k)),
                  pl.BlockSpec((None, None, window, hd), lambda b, k: (layer, b, 0, k)),
                  pl.BlockSpec((None, None, window, hd), lambda b, k: (layer, b, 0, k)),
                  pl.BlockSpec(memory_space=pl.ANY)],
        out_specs=(pl.BlockSpec((t_s, qw), lambda b, k: (rb0 + b, k)),
                   pl.BlockSpec((None, t_s, hd), lambda b, k: (b, 0, k)),
                   pl.BlockSpec((None, t_s, hd), lambda b, k: (b, 0, k))),
        input_output_aliases={6: 0},
        compiler_params=_params(("parallel", "parallel"), 8 << 20),
        name="swa_sample",
    )(sinks, z, z, z, cache_k, cache_v, o)
    return o, (kc_p, vc_p, kn_s, vn_s)


def _retention_state_update(k32, v_bf, zeta):
    kz_t = jnp.transpose(k32 * zeta).astype(BF16)
    return jnp.dot(kz_t, v_bf, preferred_element_type=F32)


def _retention_readout(q_bf, k_bf, v_bf, g32, state_bf, decay, xi, gain):
    s = lax.dot_general(q_bf, k_bf, (((1,), (1,)), ((), ())), preferred_element_type=F32) * decay
    o = (jnp.dot(s.astype(BF16), v_bf, preferred_element_type=F32)
         + jnp.dot(q_bf, state_bf, preferred_element_type=F32) * xi)
    mu = jnp.mean(o, axis=-1, keepdims=True)
    var = jnp.mean(jnp.square(o - mu), axis=-1, keepdims=True)
    on = (o - mu) * lax.rsqrt(var + GN_EPS) * gain
    return (g32 * jax.nn.sigmoid(g32)) * on


def _retention_prompt_kernel(cd_ref, q_ref, k_ref, v_ref, g_ref, decay_ref, xi_ref, zeta_ref, gain_ref,
                             o_ref, st_ref, sb_ref, *, block, unroll):
    h = pl.program_id(1)
    n_blocks = q_ref.shape[0] // block
    block_decay = cd_ref[h]
    zeta = zeta_ref[...]

    state = jnp.zeros(st_ref.shape, F32)
    for c in range(n_blocks):
        rows = slice(c * block, (c + 1) * block)
        sb_ref[c] = state.astype(BF16)
        state = block_decay * state + _retention_state_update(k_ref[rows, :], v_ref[rows, :].astype(BF16), zeta)
    st_ref[...] = state

    gain = gain_ref[pl.ds(h, 1), :]

    def body(t, carry):
        for u in range(unroll):
            c = t * unroll + u
            rows = pl.ds(pl.multiple_of(c * block, block), block)
            out = _retention_readout(q_ref[rows, :].astype(BF16), k_ref[rows, :].astype(BF16),
                                     v_ref[rows, :].astype(BF16), g_ref[rows, :], sb_ref[c],
                                     decay_ref[...], xi_ref[...], gain)
            o_ref[rows, :] = out.astype(o_ref.dtype)
        return carry

    lax.fori_loop(0, n_blocks // unroll, body, 0)


def _retention_sample_kernel(cd_ref, q_ref, k_ref, v_ref, g_ref, decay_ref, xi_ref, zeta_ref, gain_ref,
                             st_in_ref, o_in_ref, o_ref, st_ref):
    del o_in_ref
    h = pl.program_id(1)
    state = st_in_ref[...]
    k32 = k_ref[...]
    v_bf = v_ref[...].astype(BF16)
    out = _retention_readout(q_ref[...].astype(BF16), k32.astype(BF16), v_bf, g_ref[...], state.astype(BF16),
                             decay_ref[...], xi_ref[...], gain_ref[pl.ds(h, 1), :])
    o_ref[...] = out.astype(o_ref.dtype)
    st_ref[...] = cd_ref[h] * state + _retention_state_update(k32, v_bf, zeta_ref[...])


def retention(z, state_in, gain, layer, consts_p, consts_s, cfg):
    dk, dv, heads = cfg["ret_dk"], cfg["ret_dv"], cfg["ret_heads"]
    b_p, t_p, b_s, t_s = cfg["b_p"], cfg["t_p"], cfg["b_s"], cfg["t_s"]
    rows = z.shape[0]
    q0, k0 = cfg["off_q_r"] // dk, cfg["off_k_r"] // dk
    v0, g0 = cfg["off_v_r"] // dv, cfg["off_g_r"] // dv
    smem = pl.BlockSpec(memory_space=pltpu.SMEM)

    def const_specs(length):
        return [pl.BlockSpec((None, length, length), lambda b, h: (h, 0, 0)),
                pl.BlockSpec((None, length, 1), lambda b, h: (h, 0, 0)),
                pl.BlockSpec((None, length, 1), lambda b, h: (h, 0, 0)),
                pl.BlockSpec((heads, dv), lambda b, h: (0, 0))]

    decay, xi, zeta, block_decay = consts_p
    block = decay.shape[1]
    n_blocks = t_p // block
    unroll = 2 if n_blocks % 2 == 0 else 1
    need_p = (2 * (2 * t_p * dk * 4 + 2 * t_p * dv * 4 + t_p * dv * 2 + block * block * 4)
              + n_blocks * dk * dv * 2 + (8 << 20))
    rr, st_p = pl.pallas_call(
        functools.partial(_retention_prompt_kernel, block=block, unroll=unroll),
        out_shape=(jax.ShapeDtypeStruct((rows, heads * dv), BF16),
                   jax.ShapeDtypeStruct((b_p, heads, dk, dv), F32)),
        grid=(b_p, heads),
        in_specs=[smem,
                  pl.BlockSpec((t_p, dk), lambda b, h: (b, q0 + h)),
                  pl.BlockSpec((t_p, dk), lambda b, h: (b, k0 + h)),
                  pl.BlockSpec((t_p, dv), lambda b, h: (b, v0 + h)),
                  pl.BlockSpec((t_p, dv), lambda b, h: (b, g0 + h))] + const_specs(block),
        out_specs=(pl.BlockSpec((t_p, dv), lambda b, h: (b, h)),
                   pl.BlockSpec((None, None, dk, dv), lambda b, h: (b, h, 0, 0))),
        scratch_shapes=[pltpu.VMEM((n_blocks, dk, dv), BF16)],
        compiler_params=_params(("parallel", "parallel"), need_p),
        name="retention_prompt",
    )(block_decay, z, z, z, z, decay, xi, zeta, gain)

    decay, xi, zeta, block_decay = consts_s
    assert decay.shape[1] == t_s
    rb0 = (b_p * t_p) // t_s
    rr, st_s = pl.pallas_call(
        _retention_sample_kernel,
        out_shape=(jax.ShapeDtypeStruct((rows, heads * dv), BF16),
                   jax.ShapeDtypeStruct((b_s, heads, dk, dv), F32)),
        grid=(b_s, heads),
        in_specs=[smem,
                  pl.BlockSpec((t_s, dk), lambda b, h: (rb0 + b, q0 + h)),
                  pl.BlockSpec((t_s, dk), lambda b, h: (rb0 + b, k0 + h)),
                  pl.BlockSpec((t_s, dv), lambda b, h: (rb0 + b, v0 + h)),
                  pl.BlockSpec((t_s, dv), lambda b, h: (rb0 + b, g0 + h))] + const_specs(t_s) +
                 [pl.BlockSpec((None, None, None, dk, dv), lambda b, h: (layer, b, h, 0, 0)),
                  pl.BlockSpec(memory_space=pl.ANY)],
        out_specs=(pl.BlockSpec((t_s, dv), lambda b, h: (rb0 + b, h)),
                   pl.BlockSpec((None, None, dk, dv), lambda b, h: (b, h, 0, 0))),
        input_output_aliases={10: 0},
        compiler_params=_params(("parallel", "parallel"), 8 << 20),
        name="retention_sample",
    )(block_decay, z, z, z, z, decay, xi, zeta, gain, state_in, rr)
    return rr, st_p, st_s


def _merge_kernel(oa_ref, rr_ref, wa_ref, wr_ref, ga_ref, gr_ref, o_ref, *, parts):
    for rows in _row_parts(o_ref, parts):
        branch_a = jnp.dot(oa_ref[rows, :], wa_ref[...], preferred_element_type=F32)
        branch_r = jnp.dot(rr_ref[rows, :], wr_ref[...], preferred_element_type=F32)
        merged = jax.nn.sigmoid(ga_ref[rows, :]) * branch_a + jax.nn.sigmoid(gr_ref[rows, :]) * branch_r
        o_ref[rows, :] = merged.astype(o_ref.dtype)


def merge_branches(o_a, rr, w_a, w_r, layer, z, cfg):
    rows, ka = o_a.shape
    kr = rr.shape[1]
    d = w_a.shape[2]
    tm = _pick(rows, 768, 8)
    tn = _pick(d, 512, LANES)
    ga0, gr0 = cfg["off_gate_a"] // tn, cfg["off_gate_r"] // tn
    assert cfg["off_gate_a"] % tn == 0 and cfg["off_gate_r"] % tn == 0
    need = 2 * (tm * (ka + kr) * 2 + (ka + kr) * tn * 2 + 2 * tm * tn * 4 + tm * tn * 2) + 4 * tm * tn * 4
    return pl.pallas_call(
        functools.partial(_merge_kernel, parts=2 if tm % 16 == 0 else 1),
        out_shape=jax.ShapeDtypeStruct((rows, d), BF16),
        grid=(rows // tm, d // tn),
        in_specs=[pl.BlockSpec((tm, ka), lambda i, j: (i, 0)),
                  pl.BlockSpec((tm, kr), lambda i, j: (i, 0)),
                  pl.BlockSpec((None, ka, tn), lambda i, j: (layer, 0, j)),
                  pl.BlockSpec((None, kr, tn), lambda i, j: (layer, 0, j)),
                  pl.BlockSpec((tm, tn), lambda i, j: (i, ga0 + j)),
                  pl.BlockSpec((tm, tn), lambda i, j: (i, gr0 + j))],
        out_specs=pl.BlockSpec((tm, tn), lambda i, j: (i, j)),
        compiler_params=_params(("parallel", "arbitrary"), need),
        name="merge_branches",
    )(o_a, rr, w_a, w_r, z, z)


def _matmul_residual_kernel(a_ref, w_ref, x_ref, o_ref):
    o_ref[...] = x_ref[...] + jnp.dot(a_ref[...], w_ref[...], preferred_element_type=F32)


def matmul_residual(a, w, layer, x, tm_target, tn_target):
    rows, k = a.shape
    n = w.shape[2]
    tm = _pick(rows, tm_target, 8)
    tn = _pick(n, tn_target, LANES)
    need = 2 * (tm * k * 2 + k * tn * 2 + 2 * tm * tn * 4) + 2 * tm * tn * 4
    return pl.pallas_call(
        _matmul_residual_kernel,
        out_shape=jax.ShapeDtypeStruct((rows, n), F32),
        grid=(rows // tm, n // tn),
        in_specs=[pl.BlockSpec((tm, k), lambda i, j: (i, 0)),
                  pl.BlockSpec((None, k, tn), lambda i, j: (layer, 0, j)),
                  pl.BlockSpec((tm, tn), lambda i, j: (i, j))],
        out_specs=pl.BlockSpec((tm, tn), lambda i, j: (i, j)),
        compiler_params=_params(("parallel", "arbitrary"), need),
        name="matmul_residual",
    )(a, w, x)


def _gate_up_kernel(h_ref, wg_ref, wu_ref, o_ref, *, parts):
    for rows in _row_parts(o_ref, parts):
        h = h_ref[rows, :]
        gate = jnp.dot(h, wg_ref[...], preferred_element_type=F32)
        up = jnp.dot(h, wu_ref[...], preferred_element_type=F32)
        o_ref[rows, :] = ((gate * jax.nn.sigmoid(gate)) * up).astype(o_ref.dtype)


def gate_up(h, w_gate_up, layer):
    rows, d = h.shape
    f = w_gate_up.shape[2] // 2
    tm = _pick(rows, 1536, 8)
    tn = _pick(f, 256, LANES)
    nj = f // tn
    need = 2 * (tm * d * 2 + 2 * d * tn * 2 + tm * tn * 2) + 4 * tm * tn * 4
    return pl.pallas_call(
        functools.partial(_gate_up_kernel, parts=2 if tm % 16 == 0 else 1),
        out_shape=jax.ShapeDtypeStruct((rows, f), BF16),
        grid=(rows // tm, nj),
        in_specs=[pl.BlockSpec((tm, d), lambda i, j: (i, 0)),
                  pl.BlockSpec((None, d, tn), lambda i, j: (layer, 0, j)),
                  pl.BlockSpec((None, d, tn), lambda i, j: (layer, 0, nj + j))],
        out_specs=pl.BlockSpec((tm, tn), lambda i, j: (i, j)),
        compiler_params=_params(("parallel", "arbitrary"), need),
        name="swiglu_gate_up",
    )(h, w_gate_up, w_gate_up)


def _rope_tables(positions, hd):
    half = hd // 2
    inv = ROPE_THETA ** (-jnp.arange(half, dtype=F32) / half)
    ang = positions.astype(F32)[:, None] * inv[None, :]
    cos, sin = jnp.cos(ang), jnp.sin(ang)
    return jnp.concatenate([cos, cos], axis=-1), jnp.concatenate([-sin, sin], axis=-1)


def _retention_tables(heads, length):
    lg = jnp.log1p(-jnp.exp2(-5.0 - jnp.arange(heads, dtype=F32)))
    j = jnp.arange(length, dtype=F32)
    rel = j[:, None] - j[None, :]
    decay = jnp.where(rel >= 0, jnp.exp(jnp.maximum(rel, 0.0)[None] * lg[:, None, None]), 0.0)
    xi = jnp.exp((j + 1.0)[None, :] * lg[:, None])[:, :, None]
    zeta = jnp.exp((length - 1.0 - j)[None, :] * lg[:, None])[:, :, None]
    block_decay = jnp.exp(length * lg)
    return decay, xi, zeta, block_decay


def kernel(x_prompt, x_sample, cache_swa_k, cache_swa_v, state_retention, w_in, w_proj_a, w_proj_r, w_out,
           attn_sinks, ret_norm_gain, norm_mix, norm_ffn, w_gate_up, w_down, norm_final):
    b_p, t_p, d = x_prompt.shape
    b_s, t_s, _ = x_sample.shape
    depth = w_in.shape[0]
    _, _, window, kvh, hd = cache_swa_k.shape
    att_heads = attn_sinks.shape[1]
    _, _, ret_heads, ret_dk, ret_dv = state_retention.shape
    att_w, kv_w = att_heads * hd, kvh * hd
    rqk_w, rv_w = ret_heads * ret_dk, ret_heads * ret_dv
    splits = (att_w, kv_w, kv_w, rqk_w, rqk_w, rv_w, rv_w, d, d)
    offs = [0]
    for s in splits:
        offs.append(offs[-1] + s)
    assert offs[-1] == w_in.shape[2] and hd == LANES and ret_dk == LANES
    cfg = dict(b_p=b_p, t_p=t_p, b_s=b_s, t_s=t_s, head_dim=hd, kv_heads=kvh, att_group=att_heads // kvh,
               att_width=att_w, ret_heads=ret_heads, ret_dk=ret_dk, ret_dv=ret_dv,
               off_k_a=offs[1], off_v_a=offs[2], off_q_r=offs[3], off_k_r=offs[4], off_v_r=offs[5],
               off_g_r=offs[6], off_gate_a=offs[7], off_gate_r=offs[8])

    tn_in = _pick(kv_w, 512, LANES)
    assert all(o % tn_in == 0 for o in offs)
    kind_of_split = (KIND_ROPE_SCALED, KIND_ROPE, KIND_PLAIN, KIND_ROPE, KIND_ROPE_SCALED,
                     KIND_PLAIN, KIND_PLAIN, KIND_PLAIN, KIND_PLAIN)
    assert hd == ret_dk
    kinds = jnp.asarray([kd for kd, s in zip(kind_of_split, splits) for _ in range(s // tn_in)], jnp.int32)

    pos = jnp.concatenate([jnp.tile(jnp.arange(t_p, dtype=jnp.int32), b_p),
                           jnp.tile(PAST_LEN + jnp.arange(t_s, dtype=jnp.int32), b_s)])
    cos_t, sin_t = _rope_tables(pos, hd)
    ret_consts_p = _retention_tables(ret_heads, _pick(t_p, RET_BLOCK_TARGET, CHUNK))
    ret_consts_s = _retention_tables(ret_heads, t_s)

    cache_k = cache_swa_k.reshape(depth, b_s, window, kv_w)
    cache_v = cache_swa_v.reshape(depth, b_s, window, kv_w)
    w_in_b, w_a_b, w_r_b, w_out_b = (w.astype(BF16) for w in (w_in, w_proj_a, w_proj_r, w_out))
    w_gu_b, w_down_b = w_gate_up.astype(BF16), w_down.astype(BF16)

    rows_p = b_p * t_p
    x = jnp.concatenate([x_prompt.reshape(rows_p, d), x_sample.reshape(b_s * t_s, d)], axis=0)

    kp_rows, vp_rows, sp_states, ks_rows, vs_rows, ss_states = [], [], [], [], [], []
    for layer in range(depth):
        h = rmsnorm(x, norm_mix[layer], BF16)
        z = inproj(h, w_in_b, layer, kinds, cos_t, sin_t, tn_in, hd ** -0.5)
        o_a, (kc_p, vc_p, kn_s, vn_s) = swa(z, attn_sinks[layer], cache_k, cache_v, layer, cfg)
        rr, st_p, st_s = retention(z, state_retention, ret_norm_gain[layer], layer, ret_consts_p, ret_consts_s, cfg)
        merged = merge_branches(o_a, rr, w_a_b, w_r_b, layer, z, cfg)
        x = matmul_residual(merged, w_out_b, layer, x, 1536, 512)
        h = rmsnorm(x, norm_ffn[layer], BF16)
        act = gate_up(h, w_gu_b, layer)
        x = matmul_residual(act, w_down_b, layer, x, 768, 256)

        kp_rows.append(kc_p.reshape(b_p, window, kvh, hd))
        vp_rows.append(vc_p.reshape(b_p, window, kvh, hd))
        ks_rows.append(kn_s.reshape(b_s, t_s, kvh, hd))
        vs_rows.append(vn_s.reshape(b_s, t_s, kvh, hd))
        sp_states.append(st_p)
        ss_states.append(st_s)

    y_p, y_s = rmsnorm_split(x, norm_final, rows_p)
    return (y_p.reshape(b_p, t_p, d), y_s.reshape(b_s, t_s, d),
            jnp.stack(kp_rows), jnp.stack(vp_rows), jnp.stack(sp_states),
            jnp.stack(ks_rows), jnp.stack(vs_rows), jnp.stack(ss_states))
```

```python
import functools
import math

import jax
import jax.numpy as jnp
from jax import lax
from jax.experimental import pallas as pl
from jax.experimental.pallas import tpu as pltpu

F32 = jnp.float32
BF16 = jnp.bfloat16

CHUNK = 64
PAST_LEN = 2048
ROPE_THETA = 10000.0
RMS_EPS = 1e-6
GN_EPS = 1e-5

LANES = 128
SUBLANES_BF16 = 16
VMEM_BYTES_V7X = 64 * 1024 * 1024
VMEM_BUDGET = VMEM_BYTES_V7X - 8 * 1024 * 1024

RET_BLOCK_TARGET = 256


def _pick(total, target, mult):
    best = None
    for d in range(mult, min(total, target) + 1, mult):
        if total % d == 0:
            best = d
    if best is None:
        raise ValueError(f"no tile for {total} (target {target}, multiple of {mult})")
    return best


def _params(semantics, vmem_need):
    limit = min(max(int(vmem_need * 1.2), 16 * 1024 * 1024), VMEM_BUDGET)
    return pltpu.CompilerParams(dimension_semantics=semantics, vmem_limit_bytes=limit)


def _row_parts(ref, parts):
    rows = ref.shape[0]
    sub = rows // parts
    return [slice(p * sub, (p + 1) * sub) for p in range(parts)]


class SideCast:
    def __init__(self, stacked, layer, n_steps, nj):
        _, k, n = stacked.shape
        rows = min(r for r in range(SUBLANES_BF16, k + 1, SUBLANES_BF16) if k % r == 0 and k // r <= n_steps)
        last = k // rows - 1
        self.operand = stacked
        self.out_shape = jax.ShapeDtypeStruct((k, n), BF16)
        self.in_spec = pl.BlockSpec((None, rows, n), lambda i, j, *_: (layer, jnp.minimum(i * nj + j, last), 0))
        self.out_spec = pl.BlockSpec((rows, n), lambda i, j, *_: (jnp.minimum(i * nj + j, last), 0))
        self.vmem = 2 * rows * n * 6


def _run_side_casts(src_refs, dst_refs):
    for src, dst in zip(src_refs, dst_refs):
        dst[...] = src[...].astype(BF16)


def _split_refs(refs, n_in, n_out):
    n_cast = (len(refs) - n_in - n_out) // 2
    return (refs[:n_in], refs[n_in + n_cast:n_in + n_cast + n_out],
            refs[n_in:n_in + n_cast], refs[n_in + n_cast + n_out:])


def _rmsnorm_kernel(x_ref, g_ref, o_ref):
    x = x_ref[...]
    y = x * lax.rsqrt(jnp.mean(x * x, axis=-1, keepdims=True) + RMS_EPS)
    o_ref[...] = (y * g_ref[...]).astype(o_ref.dtype)


def rmsnorm(x, g, out_dtype):
    rows, d = x.shape
    tr = _pick(rows, 256, 8)
    need = 2 * tr * d * (4 + jnp.dtype(out_dtype).itemsize) + 3 * tr * d * 4
    return pl.pallas_call(
        _rmsnorm_kernel,
        out_shape=jax.ShapeDtypeStruct((rows, d), out_dtype),
        grid=(rows // tr,),
        in_specs=[pl.BlockSpec((tr, d), lambda i: (i, 0)),
                  pl.BlockSpec((1, d), lambda i: (0, 0))],
        out_specs=pl.BlockSpec((tr, d), lambda i: (i, 0)),
        compiler_params=_params(("parallel",), need),
        name="rmsnorm",
    )(x, g.reshape(1, d))


def _rmsnorm_split_kernel(x_ref, g_ref, head_ref, tail_ref, *, head_blocks):
    x = x_ref[...]
    y = x * lax.rsqrt(jnp.mean(x * x, axis=-1, keepdims=True) + RMS_EPS) * g_ref[...]
    i = pl.program_id(0)

    @pl.when(i < head_blocks)
    def _():
        head_ref[...] = y

    @pl.when(i >= head_blocks)
    def _():
        tail_ref[...] = y


def rmsnorm_split(x, g, head_rows):
    rows, d = x.shape
    assert 0 < head_rows < rows
    tr = _pick(math.gcd(head_rows, rows - head_rows), 256, 8)
    hb = head_rows // tr
    return pl.pallas_call(
        functools.partial(_rmsnorm_split_kernel, head_blocks=hb),
        out_shape=(jax.ShapeDtypeStruct((head_rows, d), F32), jax.ShapeDtypeStruct((rows - head_rows, d), F32)),
        grid=(rows // tr,),
        in_specs=[pl.BlockSpec((tr, d), lambda i: (i, 0)),
                  pl.BlockSpec((1, d), lambda i: (0, 0))],
        out_specs=(pl.BlockSpec((tr, d), lambda i: (jnp.minimum(i, hb - 1), 0)),
                   pl.BlockSpec((tr, d), lambda i: (jnp.maximum(i - hb, 0), 0))),
        compiler_params=_params(("arbitrary",), 10 * tr * d * 4),
        name="rmsnorm_split",
    )(x, g.reshape(1, d))


KIND_PLAIN, KIND_ROPE, KIND_ROPE_SCALED = 0, 1, 2


def _inproj_kernel(kind_ref, *refs, scale, parts):
    (h_ref, w_ref, cos_ref, sin_ref), (o_ref,), cast_src, cast_dst = _split_refs(refs, 4, 1)
    kind = kind_ref[pl.program_id(1)]

    @pl.when(kind == KIND_PLAIN)
    def _():
        o_ref[...] = jnp.dot(h_ref[...], w_ref[...], preferred_element_type=F32)
        _run_side_casts(cast_src, cast_dst)

    @pl.when(kind != KIND_PLAIN)
    def _():
        s = jnp.where(kind == KIND_ROPE_SCALED, F32(scale), F32(1.0))
        for rows in _row_parts(o_ref, parts):
            acc = jnp.dot(h_ref[rows, :], w_ref[...], preferred_element_type=F32)
            c = cos_ref[rows, :]
            sn = sin_ref[rows, :]
            for g in range(acc.shape[1] // LANES):
                xg = acc[:, g * LANES:(g + 1) * LANES]
                r = xg * c + pltpu.roll(xg, LANES // 2, 1) * sn
                o_ref[rows, g * LANES:(g + 1) * LANES] = r * s
        _run_side_casts(cast_src, cast_dst)


def inproj(h, w, kinds, cos_t, sin_t, tn, scale, cast_jobs=()):
    rows, d = h.shape
    n = w.shape[1]
    tm = _pick(rows, 1536, 8)
    ni, nj = rows // tm, n // tn
    casts = [SideCast(st, layer, ni * nj, nj) for st, layer in cast_jobs]
    need = (2 * (tm * d * 2 + d * tn * 2 + tm * tn * 4 + 2 * tm * LANES * 4) + 4 * tm * LANES * 4
            + sum(c.vmem for c in casts))
    grid_spec = pltpu.PrefetchScalarGridSpec(
        num_scalar_prefetch=1,
        grid=(ni, nj),
        in_specs=[pl.BlockSpec((tm, d), lambda i, j, kr: (i, 0)),
                  pl.BlockSpec((d, tn), lambda i, j, kr: (0, j)),
                  pl.BlockSpec((tm, LANES), lambda i, j, kr: (i, 0)),
                  pl.BlockSpec((tm, LANES), lambda i, j, kr: (i, 0))] + [c.in_spec for c in casts],
        out_specs=[pl.BlockSpec((tm, tn), lambda i, j, kr: (i, j))] + [c.out_spec for c in casts],
    )
    out = pl.pallas_call(
        functools.partial(_inproj_kernel, scale=scale, parts=4 if tm % 32 == 0 else 1),
        out_shape=[jax.ShapeDtypeStruct((rows, n), F32)] + [c.out_shape for c in casts],
        grid_spec=grid_spec,
        compiler_params=_params(("arbitrary", "arbitrary"), need),
        name="inproj_rope",
    )(kinds, h, w, cos_t, sin_t, *[c.operand for c in casts])
    return out[0], out[1:]


def _sink_column(sink_ref, kvh, group, rows_per_head):
    cols = [jnp.full((rows_per_head, 1), sink_ref[kvh * group + g], F32) for g in range(group)]
    return jnp.concatenate(cols, axis=0)


def _swa_chunk(q_rows, kb, vb, sink_col, group):
    hd = kb.shape[1]
    q = jnp.concatenate([q_rows[:, g * hd:(g + 1) * hd] for g in range(group)], axis=0).astype(BF16)
    s = lax.dot_general(q, kb, (((1,), (1,)), ((), ())), preferred_element_type=F32)
    m = jnp.maximum(jnp.max(s, axis=-1, keepdims=True), sink_col)
    p = jnp.exp(s - m)
    denom = jnp.sum(p, axis=-1, keepdims=True) + jnp.exp(sink_col - m)
    p = (p / denom).astype(BF16)
    return jnp.dot(p, vb, preferred_element_type=F32)


def _swa_prompt_kernel(sink_ref, q_ref, k_ref, v_ref, o_ref, kc_ref, vc_ref, kb_ref, vb_ref, s_ref, p_ref, *,
                       group, window_chunks, pad_chunks, unroll):
    seq, hd = k_ref.shape
    n_chunks = seq // CHUNK
    kvh = pl.program_id(1)
    front = (window_chunks + pad_chunks) * CHUNK
    band = front + CHUNK
    kb_ref[:front, :] = jnp.zeros((front, hd), BF16)
    vb_ref[:front, :] = jnp.zeros((front, hd), BF16)
    kb_ref[front:, :] = k_ref[...].astype(BF16)
    vb_ref[front:, :] = v_ref[...].astype(BF16)
    kc_ref[...] = k_ref[seq - kc_ref.shape[0]:, :]
    vc_ref[...] = v_ref[seq - vc_ref.shape[0]:, :]

    def chunk_rows(c):
        return pl.ds(pl.multiple_of(c * CHUNK, CHUNK), CHUNK)

    def band_rows(c):
        return pl.ds(pl.multiple_of(c * CHUNK, CHUNK), band)

    def scores(c, carry):
        q_rows = q_ref[chunk_rows(c), :]
        q = jnp.concatenate([q_rows[:, g * hd:(g + 1) * hd] for g in range(group)], axis=0).astype(BF16)
        s_ref[c] = lax.dot_general(q, kb_ref[band_rows(c), :], (((1,), (1,)), ((), ())),
                                   preferred_element_type=F32)
        return carry

    band_col = lax.broadcasted_iota(jnp.int32, (CHUNK, band), 1)

    def softmax(c, carry):
        outside = band_col < jnp.maximum(pad_chunks * CHUNK, front - c * CHUNK)
        for g in range(group):
            rows = slice(g * CHUNK, (g + 1) * CHUNK)
            sink = sink_ref[kvh * group + g]
            s = jnp.where(outside, F32(-1e30), s_ref[c, rows, :])
            m = jnp.maximum(jnp.max(s, axis=-1, keepdims=True), sink)
            p = jnp.exp(s - m)
            denom = jnp.sum(p, axis=-1, keepdims=True) + jnp.exp(sink - m)
            p_ref[c, rows, :] = (p / denom).astype(BF16)
        return carry

    def values(c, carry):
        o = jnp.dot(p_ref[c], vb_ref[band_rows(c), :], preferred_element_type=F32)
        for g in range(group):
            o_ref[chunk_rows(c), g * hd:(g + 1) * hd] = o[g * CHUNK:(g + 1) * CHUNK].astype(o_ref.dtype)
        return carry

    lax.fori_loop(0, n_chunks, scores, 0, unroll=2 * unroll)
    lax.fori_loop(0, n_chunks, softmax, 0, unroll=unroll)
    lax.fori_loop(0, n_chunks, values, 0, unroll=2 * unroll)


def _swa_sample_kernel(sink_ref, q_ref, k_ref, v_ref, ck_ref, cv_ref, o_in_ref, o_ref, kn_ref, vn_ref, *, group):
    del o_in_ref
    hd = k_ref.shape[1]
    kn_ref[...] = k_ref[...]
    vn_ref[...] = v_ref[...]
    kb = jnp.concatenate([ck_ref[...], k_ref[...]], axis=0).astype(BF16)
    vb = jnp.concatenate([cv_ref[...], v_ref[...]], axis=0).astype(BF16)
    sink_col = _sink_column(sink_ref, pl.program_id(1), group, q_ref.shape[0])
    o = _swa_chunk(q_ref[...], kb, vb, sink_col, group)
    rows = q_ref.shape[0]
    for g in range(group):
        o_ref[:, g * hd:(g + 1) * hd] = o[g * rows:(g + 1) * rows].astype(o_ref.dtype)


def swa(z, sinks, cache_k, cache_v, layer, cfg):
    hd, kvh, group = cfg["head_dim"], cfg["kv_heads"], cfg["att_group"]
    b_p, t_p, b_s, t_s = cfg["b_p"], cfg["t_p"], cfg["b_s"], cfg["t_s"]
    rows = z.shape[0]
    qw = group * hd
    k_blk0 = cfg["off_k_a"] // hd
    v_blk0 = cfg["off_v_a"] // hd
    att_w = cfg["att_width"]
    window = cache_k.shape[2]
    window_chunks = window // CHUNK
    smem = pl.BlockSpec(memory_space=pltpu.SMEM)

    lane_chunks = LANES // CHUNK
    pad_chunks = -(window_chunks + 1) % lane_chunks
    band = (window_chunks + pad_chunks + 1) * CHUNK
    front = band - CHUNK
    n_chunks = t_p // CHUNK
    need_p = (2 * (t_p * qw * 4 + 2 * t_p * hd * 4 + t_p * qw * 2) + 2 * (front + t_p) * hd * 2
              + n_chunks * group * CHUNK * band * 6 + (4 << 20))
    kv_w = kvh * hd
    o, kc_p, vc_p = pl.pallas_call(
        functools.partial(_swa_prompt_kernel, group=group, window_chunks=window_chunks, pad_chunks=pad_chunks,
                          unroll=2),
        out_shape=(jax.ShapeDtypeStruct((rows, att_w), BF16),
                   jax.ShapeDtypeStruct((b_p, window, kv_w), F32),
                   jax.ShapeDtypeStruct((b_p, window, kv_w), F32)),
        grid=(b_p, kvh),
        in_specs=[smem,
                  pl.BlockSpec((t_p, qw), lambda b, k: (b, k)),
                  pl.BlockSpec((t_p, hd), lambda b, k: (b, k_blk0 + k)),
                  pl.BlockSpec((t_p, hd), lambda b, k: (b, v_blk0 + k))],
        out_specs=(pl.BlockSpec((t_p, qw), lambda b, k: (b, k)),
                   pl.BlockSpec((None, window, hd), lambda b, k: (b, 0, k)),
                   pl.BlockSpec((None, window, hd), lambda b, k: (b, 0, k))),
        scratch_shapes=[pltpu.VMEM((front + t_p, hd), BF16), pltpu.VMEM((front + t_p, hd), BF16),
                        pltpu.VMEM((n_chunks, group * CHUNK, band), F32),
                        pltpu.VMEM((n_chunks, group * CHUNK, band), BF16)],
        compiler_params=_params(("parallel", "parallel"), need_p),
        name="swa_prompt",
    )(sinks, z, z, z)

    assert t_s == CHUNK and window + t_s == (window_chunks + 1) * CHUNK
    rb0 = (b_p * t_p) // t_s
    o, kn_s, vn_s = pl.pallas_call(
        functools.partial(_swa_sample_kernel, group=group),
        out_shape=(jax.ShapeDtypeStruct((rows, att_w), BF16),
                   jax.ShapeDtypeStruct((b_s, t_s, kv_w), F32),
                   jax.ShapeDtypeStruct((b_s, t_s, kv_w), F32)),
        grid=(b_s, kvh),
        in_specs=[smem,
                  pl.BlockSpec((t_s, qw), lambda b, k: (rb0 + b, k)),
                  pl.BlockSpec((t_s, hd), lambda b, k: (rb0 + b, k_blk0 + k)),
                  pl.BlockSpec((t_s, hd), lambda b, k: (rb0 + b, v_blk0 + k)),
                  pl.BlockSpec((None, None, window, hd), lambda b, k: (layer, b, 0, k)),
                  pl.BlockSpec((None, None, window, hd), lambda b, k: (layer, b, 0, k)),
                  pl.BlockSpec(memory_space=pl.ANY)],
        out_specs=(pl.BlockSpec((t_s, qw), lambda b, k: (rb0 + b, k)),
                   pl.BlockSpec((None, t_s, hd), lambda b, k: (b, 0, k)),
                   pl.BlockSpec((None, t_s, hd), lambda b, k: (b, 0, k))),
        input_output_aliases={6: 0},
        compiler_params=_params(("parallel", "parallel"), 8 << 20),
        name="swa_sample",
    )(sinks, z, z, z, cache_k, cache_v, o)
    return o, (kc_p, vc_p, kn_s, vn_s)


def _retention_state_update(k32, v_bf, zeta):
    kz_t = jnp.transpose(k32 * zeta).astype(BF16)
    return jnp.dot(kz_t, v_bf, preferred_element_type=F32)


def _retention_readout(q_bf, k_bf, v_bf, g32, state_bf, decay, xi, gain):
    s = lax.dot_general(q_bf, k_bf, (((1,), (1,)), ((), ())), preferred_element_type=F32) * decay
    o = (jnp.dot(s.astype(BF16), v_bf, preferred_element_type=F32)
         + jnp.dot(q_bf, state_bf, preferred_element_type=F32) * xi)
    mu = jnp.mean(o, axis=-1, keepdims=True)
    var = jnp.mean(jnp.square(o - mu), axis=-1, keepdims=True)
    on = (o - mu) * lax.rsqrt(var + GN_EPS) * gain
    return (g32 * jax.nn.sigmoid(g32)) * on


def _retention_prompt_kernel(cd_ref, q_ref, k_ref, v_ref, g_ref, decay_ref, xi_ref, zeta_ref, gain_ref,
                             o_ref, st_ref, sb_ref, *, block, unroll):
    h = pl.program_id(1)
    n_blocks = q_ref.shape[0] // block
    block_decay = cd_ref[h]
    zeta = zeta_ref[...]

    state = jnp.zeros(st_ref.shape, F32)
    for c in range(n_blocks):
        rows = slice(c * block, (c + 1) * block)
        sb_ref[c] = state.astype(BF16)
        state = block_decay * state + _retention_state_update(k_ref[rows, :], v_ref[rows, :].astype(BF16), zeta)
    st_ref[...] = state

    gain = gain_ref[pl.ds(h, 1), :]

    def body(t, carry):
        for u in range(unroll):
            c = t * unroll + u
            rows = pl.ds(pl.multiple_of(c * block, block), block)
            out = _retention_readout(q_ref[rows, :].astype(BF16), k_ref[rows, :].astype(BF16),
                                     v_ref[rows, :].astype(BF16), g_ref[rows, :], sb_ref[c],
                                     decay_ref[...], xi_ref[...], gain)
            o_ref[rows, :] = out.astype(o_ref.dtype)
        return carry

    lax.fori_loop(0, n_blocks // unroll, body, 0)


def _retention_sample_kernel(cd_ref, q_ref, k_ref, v_ref, g_ref, decay_ref, xi_ref, zeta_ref, gain_ref,
                             st_in_ref, o_in_ref, o_ref, st_ref):
    del o_in_ref
    n_heads, dk, dv = st_in_ref.shape
    head0 = pl.program_id(1) * n_heads
    for i in range(n_heads):
        qk_cols, v_cols = slice(i * dk, (i + 1) * dk), slice(i * dv, (i + 1) * dv)
        state = st_in_ref[i]
        k32 = k_ref[:, qk_cols]
        v_bf = v_ref[:, v_cols].astype(BF16)
        out = _retention_readout(q_ref[:, qk_cols].astype(BF16), k32.astype(BF16), v_bf, g_ref[:, v_cols],
                                 state.astype(BF16), decay_ref[i], xi_ref[i], gain_ref[pl.ds(head0 + i, 1), :])
        o_ref[:, v_cols] = out.astype(o_ref.dtype)
        st_ref[i] = cd_ref[head0 + i] * state + _retention_state_update(k32, v_bf, zeta_ref[i])


def retention(z, state_in, gain, layer, consts_p, consts_s, cfg):
    dk, dv, heads = cfg["ret_dk"], cfg["ret_dv"], cfg["ret_heads"]
    b_p, t_p, b_s, t_s = cfg["b_p"], cfg["t_p"], cfg["b_s"], cfg["t_s"]
    rows = z.shape[0]
    smem = pl.BlockSpec(memory_space=pltpu.SMEM)
    col_offsets = (("off_q_r", dk), ("off_k_r", dk), ("off_v_r", dv), ("off_g_r", dv))

    def specs(t, length, hg, row_block0):
        lead, hg = hg, hg or 1
        col0 = [cfg[o] // (hg * w) for o, w in col_offsets]
        return ([pl.BlockSpec((t, hg * w), lambda b, h, c0=c0: (row_block0 + b, c0 + h))
                 for c0, (_, w) in zip(col0, col_offsets)] +
                [pl.BlockSpec((lead, length, length), lambda b, h: (h, 0, 0)),
                 pl.BlockSpec((lead, length, 1), lambda b, h: (h, 0, 0)),
                 pl.BlockSpec((lead, length, 1), lambda b, h: (h, 0, 0)),
                 pl.BlockSpec((heads, dv), lambda b, h: (0, 0))])

    decay, xi, zeta, block_decay = consts_p
    block = decay.shape[1]
    n_blocks = t_p // block
    unroll = 2 if n_blocks % 2 == 0 else 1
    need_p = (2 * (2 * t_p * dk * 4 + 2 * t_p * dv * 4 + t_p * dv * 2 + block * block * 4)
              + n_blocks * dk * dv * 2 + (8 << 20))
    rr, st_p = pl.pallas_call(
        functools.partial(_retention_prompt_kernel, block=block, unroll=unroll),
        out_shape=(jax.ShapeDtypeStruct((rows, heads * dv), BF16),
                   jax.ShapeDtypeStruct((b_p, heads, dk, dv), F32)),
        grid=(b_p, heads),
        in_specs=[smem] + specs(t_p, block, None, 0),
        out_specs=(pl.BlockSpec((t_p, dv), lambda b, h: (b, h)),
                   pl.BlockSpec((None, None, dk, dv), lambda b, h: (b, h, 0, 0))),
        scratch_shapes=[pltpu.VMEM((n_blocks, dk, dv), BF16)],
        compiler_params=_params(("parallel", "parallel"), need_p),
        name="retention_prompt",
    )(block_decay, z, z, z, z, decay, xi, zeta, gain)

    decay, xi, zeta, block_decay = consts_s
    assert decay.shape[1] == t_s
    rb0 = (b_p * t_p) // t_s
    hg = max(n for n in range(1, heads + 1)
             if heads % n == 0 and all(cfg[o] % (n * w) == 0 for o, w in col_offsets))
    rr, st_s = pl.pallas_call(
        _retention_sample_kernel,
        out_shape=(jax.ShapeDtypeStruct((rows, heads * dv), BF16),
                   jax.ShapeDtypeStruct((b_s, heads, dk, dv), F32)),
        grid=(b_s, heads // hg),
        in_specs=[smem] + specs(t_s, t_s, hg, rb0) +
                 [pl.BlockSpec((None, None, hg, dk, dv), lambda b, h: (layer, b, h, 0, 0)),
                  pl.BlockSpec(memory_space=pl.ANY)],
        out_specs=(pl.BlockSpec((t_s, hg * dv), lambda b, h: (rb0 + b, h)),
                   pl.BlockSpec((None, hg, dk, dv), lambda b, h: (b, h, 0, 0))),
        input_output_aliases={10: 0},
        compiler_params=_params(("parallel", "parallel"), 16 << 20),
        name="retention_sample",
    )(block_decay, z, z, z, z, decay, xi, zeta, gain, state_in, rr)
    return rr, st_p, st_s


def _merge_kernel(*refs, parts):
    (oa_ref, rr_ref, wa_ref, wr_ref, ga_ref, gr_ref), (o_ref,), cast_src, cast_dst = _split_refs(refs, 6, 1)
    for rows in _row_parts(o_ref, parts):
        branch_a = jnp.dot(oa_ref[rows, :], wa_ref[...], preferred_element_type=F32)
        branch_r = jnp.dot(rr_ref[rows, :], wr_ref[...], preferred_element_type=F32)
        merged = jax.nn.sigmoid(ga_ref[rows, :]) * branch_a + jax.nn.sigmoid(gr_ref[rows, :]) * branch_r
        o_ref[rows, :] = merged.astype(o_ref.dtype)
    _run_side_casts(cast_src, cast_dst)


def merge_branches(o_a, rr, w_a, w_r, z, cfg, cast_jobs=()):
    rows, ka = o_a.shape
    kr = rr.shape[1]
    d = w_a.shape[1]
    tm = _pick(rows, 768, 8)
    tn = _pick(d, 512, LANES)
    ni, nj = rows // tm, d // tn
    casts = [SideCast(st, layer, ni * nj, nj) for st, layer in cast_jobs]
    ga0, gr0 = cfg["off_gate_a"] // tn, cfg["off_gate_r"] // tn
    assert cfg["off_gate_a"] % tn == 0 and cfg["off_gate_r"] % tn == 0
    need = (2 * (tm * (ka + kr) * 2 + (ka + kr) * tn * 2 + 2 * tm * tn * 4 + tm * tn * 2) + 4 * tm * tn * 4
            + sum(c.vmem for c in casts))
    out = pl.pallas_call(
        functools.partial(_merge_kernel, parts=2 if tm % 16 == 0 else 1),
        out_shape=[jax.ShapeDtypeStruct((rows, d), BF16)] + [c.out_shape for c in casts],
        grid=(ni, nj),
        in_specs=[pl.BlockSpec((tm, ka), lambda i, j: (i, 0)),
                  pl.BlockSpec((tm, kr), lambda i, j: (i, 0)),
                  pl.BlockSpec((ka, tn), lambda i, j: (0, j)),
                  pl.BlockSpec((kr, tn), lambda i, j: (0, j)),
                  pl.BlockSpec((tm, tn), lambda i, j: (i, ga0 + j)),
                  pl.BlockSpec((tm, tn), lambda i, j: (i, gr0 + j))] + [c.in_spec for c in casts],
        out_specs=[pl.BlockSpec((tm, tn), lambda i, j: (i, j))] + [c.out_spec for c in casts],
        compiler_params=_params(("arbitrary", "arbitrary"), need),
        name="merge_branches",
    )(o_a, rr, w_a, w_r, z, z, *[c.operand for c in casts])
    return out[0], out[1:]


def _matmul_residual_kernel(*refs):
    (a_ref, w_ref, x_ref), (o_ref,), cast_src, cast_dst = _split_refs(refs, 3, 1)
    o_ref[...] = x_ref[...] + jnp.dot(a_ref[...], w_ref[...], preferred_element_type=F32)
    _run_side_casts(cast_src, cast_dst)


def matmul_residual(a, w, x, tm_target, tn_target, cast_jobs=()):
    rows, k = a.shape
    n = w.shape[1]
    tm = _pick(rows, tm_target, 8)
    tn = _pick(n, tn_target, LANES)
    ni, nj = rows // tm, n // tn
    casts = [SideCast(st, layer, ni * nj, nj) for st, layer in cast_jobs]
    need = 2 * (tm * k * 2 + k * tn * 2 + 2 * tm * tn * 4) + 2 * tm * tn * 4 + sum(c.vmem for c in casts)
    out = pl.pallas_call(
        _matmul_residual_kernel,
        out_shape=[jax.ShapeDtypeStruct((rows, n), F32)] + [c.out_shape for c in casts],
        grid=(ni, nj),
        in_specs=[pl.BlockSpec((tm, k), lambda i, j: (i, 0)),
                  pl.BlockSpec((k, tn), lambda i, j: (0, j)),
                  pl.BlockSpec((tm, tn), lambda i, j: (i, j))] + [c.in_spec for c in casts],
        out_specs=[pl.BlockSpec((tm, tn), lambda i, j: (i, j))] + [c.out_spec for c in casts],
        compiler_params=_params(("arbitrary", "arbitrary"), need),
        name="matmul_residual",
    )(a, w, x, *[c.operand for c in casts])
    return out[0], out[1:]


def _gate_up_kernel(*refs, parts):
    (h_ref, wg_ref, wu_ref), (o_ref,), cast_src, cast_dst = _split_refs(refs, 3, 1)
    for rows in _row_parts(o_ref, parts):
        h = h_ref[rows, :]
        gate = jnp.dot(h, wg_ref[...], preferred_element_type=F32)
        up = jnp.dot(h, wu_ref[...], preferred_element_type=F32)
        o_ref[rows, :] = ((gate * jax.nn.sigmoid(gate)) * up).astype(o_ref.dtype)
    _run_side_casts(cast_src, cast_dst)


def gate_up(h, w_gate_up, cast_jobs=()):
    rows, d = h.shape
    f = w_gate_up.shape[1] // 2
    tm = _pick(rows, 1536, 8)
    tn = _pick(f, 256, LANES)
    ni, nj = rows // tm, f // tn
    casts = [SideCast(st, layer, ni * nj, nj) for st, layer in cast_jobs]
    need = 2 * (tm * d * 2 + 2 * d * tn * 2 + tm * tn * 2) + 4 * tm * tn * 4 + sum(c.vmem for c in casts)
    out = pl.pallas_call(
        functools.partial(_gate_up_kernel, parts=2 if tm % 16 == 0 else 1),
        out_shape=[jax.ShapeDtypeStruct((rows, f), BF16)] + [c.out_shape for c in casts],
        grid=(ni, nj),
        in_specs=[pl.BlockSpec((tm, d), lambda i, j: (i, 0)),
                  pl.BlockSpec((d, tn), lambda i, j: (0, j)),
                  pl.BlockSpec((d, tn), lambda i, j: (0, nj + j))] + [c.in_spec for c in casts],
        out_specs=[pl.BlockSpec((tm, tn), lambda i, j: (i, j))] + [c.out_spec for c in casts],
        compiler_params=_params(("arbitrary", "arbitrary"), need),
        name="swiglu_gate_up",
    )(h, w_gate_up, w_gate_up, *[c.operand for c in casts])
    return out[0], out[1:]


def _rope_tables(positions, hd):
    half = hd // 2
    inv = ROPE_THETA ** (-jnp.arange(half, dtype=F32) / half)
    ang = positions.astype(F32)[:, None] * inv[None, :]
    cos, sin = jnp.cos(ang), jnp.sin(ang)
    return jnp.concatenate([cos, cos], axis=-1), jnp.concatenate([-sin, sin], axis=-1)


def _retention_tables(heads, length):
    lg = jnp.log1p(-jnp.exp2(-5.0 - jnp.arange(heads, dtype=F32)))
    j = jnp.arange(length, dtype=F32)
    rel = j[:, None] - j[None, :]
    decay = jnp.where(rel >= 0, jnp.exp(jnp.maximum(rel, 0.0)[None] * lg[:, None, None]), 0.0)
    xi = jnp.exp((j + 1.0)[None, :] * lg[:, None])[:, :, None]
    zeta = jnp.exp((length - 1.0 - j)[None, :] * lg[:, None])[:, :, None]
    block_decay = jnp.exp(length * lg)
    return decay, xi, zeta, block_decay


def kernel(x_prompt, x_sample, cache_swa_k, cache_swa_v, state_retention, w_in, w_proj_a, w_proj_r, w_out,
           attn_sinks, ret_norm_gain, norm_mix, norm_ffn, w_gate_up, w_down, norm_final):
    b_p, t_p, d = x_prompt.shape
    b_s, t_s, _ = x_sample.shape
    depth = w_in.shape[0]
    _, _, window, kvh, hd = cache_swa_k.shape
    att_heads = attn_sinks.shape[1]
    _, _, ret_heads, ret_dk, ret_dv = state_retention.shape
    att_w, kv_w = att_heads * hd, kvh * hd
    rqk_w, rv_w = ret_heads * ret_dk, ret_heads * ret_dv
    splits = (att_w, kv_w, kv_w, rqk_w, rqk_w, rv_w, rv_w, d, d)
    offs = [0]
    for s in splits:
        offs.append(offs[-1] + s)
    assert offs[-1] == w_in.shape[2] and hd == LANES and ret_dk == LANES
    cfg = dict(b_p=b_p, t_p=t_p, b_s=b_s, t_s=t_s, head_dim=hd, kv_heads=kvh, att_group=att_heads // kvh,
               att_width=att_w, ret_heads=ret_heads, ret_dk=ret_dk, ret_dv=ret_dv,
               off_k_a=offs[1], off_v_a=offs[2], off_q_r=offs[3], off_k_r=offs[4], off_v_r=offs[5],
               off_g_r=offs[6], off_gate_a=offs[7], off_gate_r=offs[8])

    tn_in = _pick(kv_w, 512, LANES)
    assert all(o % tn_in == 0 for o in offs)
    kind_of_split = (KIND_ROPE_SCALED, KIND_ROPE, KIND_PLAIN, KIND_ROPE, KIND_ROPE_SCALED,
                     KIND_PLAIN, KIND_PLAIN, KIND_PLAIN, KIND_PLAIN)
    assert hd == ret_dk
    kinds = jnp.asarray([kd for kd, s in zip(kind_of_split, splits) for _ in range(s // tn_in)], jnp.int32)

    pos = jnp.concatenate([jnp.tile(jnp.arange(t_p, dtype=jnp.int32), b_p),
                           jnp.tile(PAST_LEN + jnp.arange(t_s, dtype=jnp.int32), b_s)])
    cos_t, sin_t = _rope_tables(pos, hd)
    ret_consts_p = _retention_tables(ret_heads, _pick(t_p, RET_BLOCK_TARGET, CHUNK))
    ret_consts_s = _retention_tables(ret_heads, t_s)

    cache_k = cache_swa_k.reshape(depth, b_s, window, kv_w)
    cache_v = cache_swa_v.reshape(depth, b_s, window, kv_w)
    w_in_b, w_a_b, w_r_b, w_out_b, w_down_b = (w[0].astype(BF16) for w in (w_in, w_proj_a, w_proj_r, w_out, w_down))

    rows_p = b_p * t_p
    x = jnp.concatenate([x_prompt.reshape(rows_p, d), x_sample.reshape(b_s * t_s, d)], axis=0)

    kp_rows, vp_rows, sp_states, ks_rows, vs_rows, ss_states = [], [], [], [], [], []
    for layer in range(depth):
        nxt = layer + 1

        def next_layer(*stacked):
            return [(w, nxt) for w in stacked] if nxt < depth else []

        h = rmsnorm(x, norm_mix[layer], BF16)
        z, (w_gu_b,) = inproj(h, w_in_b, kinds, cos_t, sin_t, tn_in, hd ** -0.5, [(w_gate_up, layer)])
        o_a, (kc_p, vc_p, kn_s, vn_s) = swa(z, attn_sinks[layer], cache_k, cache_v, layer, cfg)
        rr, st_p, st_s = retention(z, state_retention, ret_norm_gain[layer], layer, ret_consts_p, ret_consts_s, cfg)
        merged, next_proj = merge_branches(o_a, rr, w_a_b, w_r_b, z, cfg, next_layer(w_proj_a, w_proj_r))
        x, next_out = matmul_residual(merged, w_out_b, x, 1536, 512, next_layer(w_out))
        h = rmsnorm(x, norm_ffn[layer], BF16)
        act, next_in = gate_up(h, w_gu_b, next_layer(w_in))
        x, next_down = matmul_residual(act, w_down_b, x, 768, 256, next_layer(w_down))
        if nxt < depth:
            (w_a_b, w_r_b), (w_out_b,), (w_in_b,), (w_down_b,) = next_proj, next_out, next_in, next_down

        kp_rows.append(kc_p.reshape(b_p, window, kvh, hd))
        vp_rows.append(vc_p.reshape(b_p, window, kvh, hd))
        ks_rows.append(kn_s.reshape(b_s, t_s, kvh, hd))
        vs_rows.append(vn_s.reshape(b_s, t_s, kvh, hd))
        sp_states.append(st_p)
        ss_states.append(st_s)

    y_p, y_s = rmsnorm_split(x, norm_final, rows_p)
    return (y_p.reshape(b_p, t_p, d), y_s.reshape(b_s, t_s, d),
            jnp.stack(kp_rows), jnp.stack(vp_rows), jnp.stack(sp_states),
            jnp.stack(ks_rows), jnp.stack(vs_rows), jnp.stack(ss_states))
```

```python
import functools
import math

import jax
import jax.numpy as jnp
from jax import lax
from jax.experimental import pallas as pl
from jax.experimental.pallas import tpu as pltpu

F32 = jnp.float32
BF16 = jnp.bfloat16

CHUNK = 64
PAST_LEN = 2048
ROPE_THETA = 10000.0
RMS_EPS = 1e-6
GN_EPS = 1e-5

LANES = 128
SUBLANES_BF16 = 16
VMEM_BYTES_V7X = 64 * 1024 * 1024
VMEM_BUDGET = VMEM_BYTES_V7X - 8 * 1024 * 1024

RET_BLOCK_TARGET = 256


def _pick(total, target, mult):
    best = None
    for d in range(mult, min(total, target) + 1, mult):
        if total % d == 0:
            best = d
    if best is None:
        raise ValueError(f"no tile for {total} (target {target}, multiple of {mult})")
    return best


def _params(semantics, vmem_need):
    limit = min(max(int(vmem_need * 1.2), 16 * 1024 * 1024), VMEM_BUDGET)
    return pltpu.CompilerParams(dimension_semantics=semantics, vmem_limit_bytes=limit)


def _row_parts(ref, parts):
    rows = ref.shape[0]
    sub = rows // parts
    return [slice(p * sub, (p + 1) * sub) for p in range(parts)]


class SideCast:
    def __init__(self, stacked, layer, n_steps, nj):
        _, k, n = stacked.shape
        rows = min(r for r in range(SUBLANES_BF16, k + 1, SUBLANES_BF16) if k % r == 0 and k // r <= n_steps)
        last = k // rows - 1
        self.operand = stacked
        self.out_shape = jax.ShapeDtypeStruct((k, n), BF16)
        self.in_spec = pl.BlockSpec((None, rows, n), lambda i, j, *_: (layer, jnp.minimum(i * nj + j, last), 0))
        self.out_spec = pl.BlockSpec((rows, n), lambda i, j, *_: (jnp.minimum(i * nj + j, last), 0))
        self.vmem = 2 * rows * n * 6


def _run_side_casts(src_refs, dst_refs):
    for src, dst in zip(src_refs, dst_refs):
        dst[...] = src[...].astype(BF16)


def _split_refs(refs, n_in, n_out, n_scratch=0):
    n_cast = (len(refs) - n_in - n_out - n_scratch) // 2
    a, b, c = n_in + n_cast, n_in + n_cast + n_out, n_in + 2 * n_cast + n_out
    return refs[:n_in], refs[a:b], refs[n_in:a], refs[b:c], refs[c:]


def _lane_group_sum(x):
    acc = x[:, :LANES]
    for t in range(1, x.shape[1] // LANES):
        acc = acc + x[:, t * LANES:(t + 1) * LANES]
    return acc


def _store_row_scale(ssq_ref, scale_ref, d_model):
    @pl.when(pl.program_id(1) == 0)
    def _():
        mean_sq = jnp.sum(ssq_ref[...], axis=-1, keepdims=True) / d_model
        scale_ref[...] = jnp.broadcast_to(lax.rsqrt(mean_sq + RMS_EPS), scale_ref.shape)


def _norm_prep_kernel(xp_ref, xs_ref, g_ref, x_ref, xg_ref, ssq_ref, *, head_blocks):
    def emit(src_ref):
        x = src_ref[...]
        x_ref[...] = x
        xg_ref[...] = (x * g_ref[...]).astype(BF16)
        ssq_ref[...] = _lane_group_sum(x * x)

    i = pl.program_id(0)
    pl.when(i < head_blocks)(functools.partial(emit, xp_ref))
    pl.when(i >= head_blocks)(functools.partial(emit, xs_ref))


def norm_prep(x_head, x_tail, g):
    head_rows, d = x_head.shape
    tail_rows = x_tail.shape[0]
    rows = head_rows + tail_rows
    tr = _pick(math.gcd(head_rows, tail_rows), 256, 8)
    hb = head_rows // tr
    return pl.pallas_call(
        functools.partial(_norm_prep_kernel, head_blocks=hb),
        out_shape=(jax.ShapeDtypeStruct((rows, d), F32), jax.ShapeDtypeStruct((rows, d), BF16),
                   jax.ShapeDtypeStruct((rows, LANES), F32)),
        grid=(rows // tr,),
        in_specs=[pl.BlockSpec((tr, d), lambda i: (jnp.minimum(i, hb - 1), 0)),
                  pl.BlockSpec((tr, d), lambda i: (jnp.maximum(i - hb, 0), 0)),
                  pl.BlockSpec((1, d), lambda i: (0, 0))],
        out_specs=(pl.BlockSpec((tr, d), lambda i: (i, 0)),
                   pl.BlockSpec((tr, d), lambda i: (i, 0)),
                   pl.BlockSpec((tr, LANES), lambda i: (i, 0))),
        compiler_params=_params(("arbitrary",), 16 * tr * d * 4),
        name="norm_prep",
    )(x_head, x_tail, g.reshape(1, d))


def _rmsnorm_split_kernel(x_ref, g_ref, head_ref, tail_ref, *, head_blocks):
    x = x_ref[...]
    y = x * lax.rsqrt(jnp.mean(x * x, axis=-1, keepdims=True) + RMS_EPS) * g_ref[...]
    i = pl.program_id(0)

    @pl.when(i < head_blocks)
    def _():
        head_ref[...] = y

    @pl.when(i >= head_blocks)
    def _():
        tail_ref[...] = y


def rmsnorm_split(x, g, head_rows):
    rows, d = x.shape
    assert 0 < head_rows < rows
    tr = _pick(math.gcd(head_rows, rows - head_rows), 256, 8)
    hb = head_rows // tr
    return pl.pallas_call(
        functools.partial(_rmsnorm_split_kernel, head_blocks=hb),
        out_shape=(jax.ShapeDtypeStruct((head_rows, d), F32), jax.ShapeDtypeStruct((rows - head_rows, d), F32)),
        grid=(rows // tr,),
        in_specs=[pl.BlockSpec((tr, d), lambda i: (i, 0)),
                  pl.BlockSpec((1, d), lambda i: (0, 0))],
        out_specs=(pl.BlockSpec((tr, d), lambda i: (jnp.minimum(i, hb - 1), 0)),
                   pl.BlockSpec((tr, d), lambda i: (jnp.maximum(i - hb, 0), 0))),
        compiler_params=_params(("arbitrary",), 10 * tr * d * 4),
        name="rmsnorm_split",
    )(x, g.reshape(1, d))


KIND_PLAIN, KIND_ROPE, KIND_ROPE_SCALED = 0, 1, 2


def _inproj_kernel(kind_ref, *refs, scale, parts, d_model):
    ((xg_ref, w_ref, cos_ref, sin_ref, ssq_ref), (o_ref,), cast_src, cast_dst,
     (rinv_ref,)) = _split_refs(refs, 5, 1, 1)
    kind = kind_ref[pl.program_id(1)]
    _store_row_scale(ssq_ref, rinv_ref, d_model)
    groups = range(o_ref.shape[1] // LANES)

    @pl.when(kind == KIND_PLAIN)
    def _():
        for rows in _row_parts(o_ref, parts):
            acc = jnp.dot(xg_ref[rows, :], w_ref[...], preferred_element_type=F32)
            rinv = rinv_ref[rows, :]
            for g in groups:
                o_ref[rows, g * LANES:(g + 1) * LANES] = acc[:, g * LANES:(g + 1) * LANES] * rinv
        _run_side_casts(cast_src, cast_dst)

    @pl.when(kind != KIND_PLAIN)
    def _():
        s = jnp.where(kind == KIND_ROPE_SCALED, F32(scale), F32(1.0))
        for rows in _row_parts(o_ref, parts):
            acc = jnp.dot(xg_ref[rows, :], w_ref[...], preferred_element_type=F32)
            rinv = rinv_ref[rows, :]
            c = cos_ref[rows, :]
            sn = sin_ref[rows, :]
            for g in groups:
                xn = acc[:, g * LANES:(g + 1) * LANES] * rinv
                r = xn * c + pltpu.roll(xn, LANES // 2, 1) * sn
                o_ref[rows, g * LANES:(g + 1) * LANES] = r * s
        _run_side_casts(cast_src, cast_dst)


def inproj(xg, ssq, w, kinds, cos_t, sin_t, tn, scale, cast_jobs=()):
    rows, d = xg.shape
    n = w.shape[1]
    tm = _pick(rows, 1536, 8)
    ni, nj = rows // tm, n // tn
    casts = [SideCast(st, layer, ni * nj, nj) for st, layer in cast_jobs]
    need = (2 * (tm * d * 2 + d * tn * 2 + tm * tn * 4 + 3 * tm * LANES * 4) + 5 * tm * LANES * 4
            + sum(c.vmem for c in casts))
    grid_spec = pltpu.PrefetchScalarGridSpec(
        num_scalar_prefetch=1,
        grid=(ni, nj),
        in_specs=[pl.BlockSpec((tm, d), lambda i, j, kr: (i, 0)),
                  pl.BlockSpec((d, tn), lambda i, j, kr: (0, j)),
                  pl.BlockSpec((tm, LANES), lambda i, j, kr: (i, 0)),
                  pl.BlockSpec((tm, LANES), lambda i, j, kr: (i, 0)),
                  pl.BlockSpec((tm, LANES), lambda i, j, kr: (i, 0))] + [c.in_spec for c in casts],
        out_specs=[pl.BlockSpec((tm, tn), lambda i, j, kr: (i, j))] + [c.out_spec for c in casts],
        scratch_shapes=[pltpu.VMEM((tm, LANES), F32)],
    )
    out = pl.pallas_call(
        functools.partial(_inproj_kernel, scale=scale, parts=4 if tm % 32 == 0 else 1, d_model=d),
        out_shape=[jax.ShapeDtypeStruct((rows, n), F32)] + [c.out_shape for c in casts],
        grid_spec=grid_spec,
        compiler_params=_params(("arbitrary", "arbitrary"), need),
        name="inproj_rope",
    )(kinds, xg, w, cos_t, sin_t, ssq, *[c.operand for c in casts])
    return out[0], out[1:]


def _sink_column(sink_ref, kvh, group, rows_per_head):
    cols = [jnp.full((rows_per_head, 1), sink_ref[kvh * group + g], F32) for g in range(group)]
    return jnp.concatenate(cols, axis=0)


def _swa_chunk(q_rows, kb, vb, sink_col, group):
    hd = kb.shape[1]
    q = jnp.concatenate([q_rows[:, g * hd:(g + 1) * hd] for g in range(group)], axis=0).astype(BF16)
    s = lax.dot_general(q, kb, (((1,), (1,)), ((), ())), preferred_element_type=F32)
    m = jnp.maximum(jnp.max(s, axis=-1, keepdims=True), sink_col)
    p = jnp.exp(s - m)
    denom = jnp.sum(p, axis=-1, keepdims=True) + jnp.exp(sink_col - m)
    p = (p / denom).astype(BF16)
    return jnp.dot(p, vb, preferred_element_type=F32)


def _swa_prompt_kernel(sink_ref, q_ref, k_ref, v_ref, o_ref, kc_ref, vc_ref, kb_ref, vb_ref, s_ref, p_ref, *,
                       group, window_chunks, pad_chunks, unroll):
    seq, hd = k_ref.shape
    n_chunks = seq // CHUNK
    kvh = pl.program_id(1)
    front = (window_chunks + pad_chunks) * CHUNK
    band = front + CHUNK
    kb_ref[:front, :] = jnp.zeros((front, hd), BF16)
    vb_ref[:front, :] = jnp.zeros((front, hd), BF16)
    kb_ref[front:, :] = k_ref[...].astype(BF16)
    vb_ref[front:, :] = v_ref[...].astype(BF16)
    kc_ref[...] = k_ref[seq - kc_ref.shape[0]:, :]
    vc_ref[...] = v_ref[seq - vc_ref.shape[0]:, :]

    def chunk_rows(c):
        return pl.ds(pl.multiple_of(c * CHUNK, CHUNK), CHUNK)

    def band_rows(c):
        return pl.ds(pl.multiple_of(c * CHUNK, CHUNK), band)

    def scores(c, carry):
        q_rows = q_ref[chunk_rows(c), :]
        q = jnp.concatenate([q_rows[:, g * hd:(g + 1) * hd] for g in range(group)], axis=0).astype(BF16)
        s_ref[c] = lax.dot_general(q, kb_ref[band_rows(c), :], (((1,), (1,)), ((), ())),
                                   preferred_element_type=F32)
        return carry

    band_col = lax.broadcasted_iota(jnp.int32, (CHUNK, band), 1)

    def softmax(c, carry):
        outside = band_col < jnp.maximum(pad_chunks * CHUNK, front - c * CHUNK)
        for g in range(group):
            rows = slice(g * CHUNK, (g + 1) * CHUNK)
            sink = sink_ref[kvh * group + g]
            s = jnp.where(outside, F32(-1e30), s_ref[c, rows, :])
            m = jnp.maximum(jnp.max(s, axis=-1, keepdims=True), sink)
            p = jnp.exp(s - m)
            denom = jnp.sum(p, axis=-1, keepdims=True) + jnp.exp(sink - m)
            p_ref[c, rows, :] = (p / denom).astype(BF16)
        return carry

    def values(c, carry):
        o = jnp.dot(p_ref[c], vb_ref[band_rows(c), :], preferred_element_type=F32)
        for g in range(group):
            o_ref[chunk_rows(c), g * hd:(g + 1) * hd] = o[g * CHUNK:(g + 1) * CHUNK].astype(o_ref.dtype)
        return carry

    lax.fori_loop(0, n_chunks, scores, 0, unroll=2 * unroll)
    lax.fori_loop(0, n_chunks, softmax, 0, unroll=unroll)
    lax.fori_loop(0, n_chunks, values, 0, unroll=2 * unroll)


def _swa_sample_kernel(sink_ref, q_ref, k_ref, v_ref, ck_ref, cv_ref, o_in_ref, o_ref, kn_ref, vn_ref, *, group):
    del o_in_ref
    hd = k_ref.shape[1]
    kn_ref[...] = k_ref[...]
    vn_ref[...] = v_ref[...]
    kb = jnp.concatenate([ck_ref[...], k_ref[...]], axis=0).astype(BF16)
    vb = jnp.concatenate([cv_ref[...], v_ref[...]], axis=0).astype(BF16)
    sink_col = _sink_column(sink_ref, pl.program_id(1), group, q_ref.shape[0])
    o = _swa_chunk(q_ref[...], kb, vb, sink_col, group)
    rows = q_ref.shape[0]
    for g in range(group):
        o_ref[:, g * hd:(g + 1) * hd] = o[g * rows:(g + 1) * rows].astype(o_ref.dtype)


def swa(z, sinks, cache_k, cache_v, layer, cfg):
    hd, kvh, group = cfg["head_dim"], cfg["kv_heads"], cfg["att_group"]
    b_p, t_p, b_s, t_s = cfg["b_p"], cfg["t_p"], cfg["b_s"], cfg["t_s"]
    rows = z.shape[0]
    qw = group * hd
    k_blk0 = cfg["off_k_a"] // hd
    v_blk0 = cfg["off_v_a"] // hd
    att_w = cfg["att_width"]
    window = cache_k.shape[2]
    window_chunks = window // CHUNK
    smem = pl.BlockSpec(memory_space=pltpu.SMEM)

    lane_chunks = LANES // CHUNK
    pad_chunks = -(window_chunks + 1) % lane_chunks
    band = (window_chunks + pad_chunks + 1) * CHUNK
    front = band - CHUNK
    n_chunks = t_p // CHUNK
    need_p = (2 * (t_p * qw * 4 + 2 * t_p * hd * 4 + t_p * qw * 2) + 2 * (front + t_p) * hd * 2
              + n_chunks * group * CHUNK * band * 6 + (4 << 20))
    kv_w = kvh * hd
    o, kc_p, vc_p = pl.pallas_call(
        functools.partial(_swa_prompt_kernel, group=group, window_chunks=window_chunks, pad_chunks=pad_chunks,
                          unroll=2),
        out_shape=(jax.ShapeDtypeStruct((rows, att_w), BF16),
                   jax.ShapeDtypeStruct((b_p, window, kv_w), F32),
                   jax.ShapeDtypeStruct((b_p, window, kv_w), F32)),
        grid=(b_p, kvh),
        in_specs=[smem,
                  pl.BlockSpec((t_p, qw), lambda b, k: (b, k)),
                  pl.BlockSpec((t_p, hd), lambda b, k: (b, k_blk0 + k)),
                  pl.BlockSpec((t_p, hd), lambda b, k: (b, v_blk0 + k))],
        out_specs=(pl.BlockSpec((t_p, qw), lambda b, k: (b, k)),
                   pl.BlockSpec((None, window, hd), lambda b, k: (b, 0, k)),
                   pl.BlockSpec((None, window, hd), lambda b, k: (b, 0, k))),
        scratch_shapes=[pltpu.VMEM((front + t_p, hd), BF16), pltpu.VMEM((front + t_p, hd), BF16),
                        pltpu.VMEM((n_chunks, group * CHUNK, band), F32),
                        pltpu.VMEM((n_chunks, group * CHUNK, band), BF16)],
        compiler_params=_params(("parallel", "parallel"), need_p),
        name="swa_prompt",
    )(sinks, z, z, z)

    assert t_s == CHUNK and window + t_s == (window_chunks + 1) * CHUNK
    rb0 = (b_p * t_p) // t_s
    o, kn_s, vn_s = pl.pallas_call(
        functools.partial(_swa_sample_kernel, group=group),
        out_shape=(jax.ShapeDtypeStruct((rows, att_w), BF16),
                   jax.ShapeDtypeStruct((b_s, t_s, kv_w), F32),
                   jax.ShapeDtypeStruct((b_s, t_s, kv_w), F32)),
        grid=(b_s, kvh),
        in_specs=[smem,
                  pl.BlockSpec((t_s, qw), lambda b, k: (rb0 + b, k)),
                  pl.BlockSpec((t_s, hd), lambda b, k: (rb0 + b, k_blk0 + k)),
                  pl.BlockSpec((t_s, hd), lambda b, k: (rb0 + b, v_blk0 + k)),
                  pl.BlockSpec((None, None, window, hd), lambda b, k: (layer, b, 0, k)),
                  pl.BlockSpec((None, None, window, hd), lambda b, k: (layer, b, 0, k)),
                  pl.BlockSpec(memory_space=pl.ANY)],
        out_specs=(pl.BlockSpec((t_s, qw), lambda b, k: (rb0 + b, k)),
                   pl.BlockSpec((None, t_s, hd), lambda b, k: (b, 0, k)),
                   pl.BlockSpec((None, t_s, hd), lambda b, k: (b, 0, k))),
        input_output_aliases={6: 0},
        compiler_params=_params(("parallel", "parallel"), 8 << 20),
        name="swa_sample",
    )(sinks, z, z, z, cache_k, cache_v, o)
    return o, (kc_p, vc_p, kn_s, vn_s)


def _retention_state_update(k32, v_bf, zeta):
    kz_t = jnp.transpose(k32 * zeta).astype(BF16)
    return jnp.dot(kz_t, v_bf, preferred_element_type=F32)


def _retention_readout(q_bf, k_bf, v_bf, g32, state_bf, decay, xi, gain):
    s = lax.dot_general(q_bf, k_bf, (((1,), (1,)), ((), ())), preferred_element_type=F32) * decay
    o = (jnp.dot(s.astype(BF16), v_bf, preferred_element_type=F32)
         + jnp.dot(q_bf, state_bf, preferred_element_type=F32) * xi)
    mu = jnp.mean(o, axis=-1, keepdims=True)
    var = jnp.mean(jnp.square(o - mu), axis=-1, keepdims=True)
    on = (o - mu) * lax.rsqrt(var + GN_EPS) * gain
    return (g32 * jax.nn.sigmoid(g32)) * on


def _retention_prompt_kernel(cd_ref, q_ref, k_ref, v_ref, g_ref, decay_ref, xi_ref, zeta_ref, gain_ref,
                             o_ref, st_ref, sb_ref, *, block, unroll):
    h = pl.program_id(1)
    n_blocks = q_ref.shape[0] // block
    block_decay = cd_ref[h]
    zeta = zeta_ref[...]

    state = jnp.zeros(st_ref.shape, F32)
    for c in range(n_blocks):
        rows = slice(c * block, (c + 1) * block)
        sb_ref[c] = state.astype(BF16)
        state = block_decay * state + _retention_state_update(k_ref[rows, :], v_ref[rows, :].astype(BF16), zeta)
    st_ref[...] = state

    gain = gain_ref[pl.ds(h, 1), :]

    def body(t, carry):
        for u in range(unroll):
            c = t * unroll + u
            rows = pl.ds(pl.multiple_of(c * block, block), block)
            out = _retention_readout(q_ref[rows, :].astype(BF16), k_ref[rows, :].astype(BF16),
                                     v_ref[rows, :].astype(BF16), g_ref[rows, :], sb_ref[c],
                                     decay_ref[...], xi_ref[...], gain)
            o_ref[rows, :] = out.astype(o_ref.dtype)
        return carry

    lax.fori_loop(0, n_blocks // unroll, body, 0)


def _retention_sample_kernel(cd_ref, q_ref, k_ref, v_ref, g_ref, decay_ref, xi_ref, zeta_ref, gain_ref,
                             st_in_ref, o_in_ref, o_ref, st_ref):
    del o_in_ref
    n_heads, dk, dv = st_in_ref.shape
    head0 = pl.program_id(1) * n_heads
    for i in range(n_heads):
        qk_cols, v_cols = slice(i * dk, (i + 1) * dk), slice(i * dv, (i + 1) * dv)
        state = st_in_ref[i]
        k32 = k_ref[:, qk_cols]
        v_bf = v_ref[:, v_cols].astype(BF16)
        out = _retention_readout(q_ref[:, qk_cols].astype(BF16), k32.astype(BF16), v_bf, g_ref[:, v_cols],
                                 state.astype(BF16), decay_ref[i], xi_ref[i], gain_ref[pl.ds(head0 + i, 1), :])
        o_ref[:, v_cols] = out.astype(o_ref.dtype)
        st_ref[i] = cd_ref[head0 + i] * state + _retention_state_update(k32, v_bf, zeta_ref[i])


def retention(z, state_in, gain, layer, consts_p, consts_s, cfg):
    dk, dv, heads = cfg["ret_dk"], cfg["ret_dv"], cfg["ret_heads"]
    b_p, t_p, b_s, t_s = cfg["b_p"], cfg["t_p"], cfg["b_s"], cfg["t_s"]
    rows = z.shape[0]
    smem = pl.BlockSpec(memory_space=pltpu.SMEM)
    col_offsets = (("off_q_r", dk), ("off_k_r", dk), ("off_v_r", dv), ("off_g_r", dv))

    def specs(t, length, hg, row_block0):
        lead, hg = hg, hg or 1
        col0 = [cfg[o] // (hg * w) for o, w in col_offsets]
        return ([pl.BlockSpec((t, hg * w), lambda b, h, c0=c0: (row_block0 + b, c0 + h))
                 for c0, (_, w) in zip(col0, col_offsets)] +
                [pl.BlockSpec((lead, length, length), lambda b, h: (h, 0, 0)),
                 pl.BlockSpec((lead, length, 1), lambda b, h: (h, 0, 0)),
                 pl.BlockSpec((lead, length, 1), lambda b, h: (h, 0, 0)),
                 pl.BlockSpec((heads, dv), lambda b, h: (0, 0))])

    decay, xi, zeta, block_decay = consts_p
    block = decay.shape[1]
    n_blocks = t_p // block
    unroll = max(u for u in (1, 2, 4, 8) if n_blocks % u == 0)
    need_p = (2 * (2 * t_p * dk * 4 + 2 * t_p * dv * 4 + t_p * dv * 2 + block * block * 4)
              + n_blocks * dk * dv * 2 + (8 << 20))
    rr, st_p = pl.pallas_call(
        functools.partial(_retention_prompt_kernel, block=block, unroll=unroll),
        out_shape=(jax.ShapeDtypeStruct((rows, heads * dv), BF16),
                   jax.ShapeDtypeStruct((b_p, heads, dk, dv), F32)),
        grid=(b_p, heads),
        in_specs=[smem] + specs(t_p, block, None, 0),
        out_specs=(pl.BlockSpec((t_p, dv), lambda b, h: (b, h)),
                   pl.BlockSpec((None, None, dk, dv), lambda b, h: (b, h, 0, 0))),
        scratch_shapes=[pltpu.VMEM((n_blocks, dk, dv), BF16)],
        compiler_params=_params(("parallel", "parallel"), need_p),
        name="retention_prompt",
    )(block_decay, z, z, z, z, decay, xi, zeta, gain)

    decay, xi, zeta, block_decay = consts_s
    assert decay.shape[1] == t_s
    rb0 = (b_p * t_p) // t_s
    hg = max(n for n in range(1, heads + 1)
             if heads % n == 0 and all(cfg[o] % (n * w) == 0 for o, w in col_offsets))
    rr, st_s = pl.pallas_call(
        _retention_sample_kernel,
        out_shape=(jax.ShapeDtypeStruct((rows, heads * dv), BF16),
                   jax.ShapeDtypeStruct((b_s, heads, dk, dv), F32)),
        grid=(b_s, heads // hg),
        in_specs=[smem] + specs(t_s, t_s, hg, rb0) +
                 [pl.BlockSpec((None, None, hg, dk, dv), lambda b, h: (layer, b, h, 0, 0)),
                  pl.BlockSpec(memory_space=pl.ANY)],
        out_specs=(pl.BlockSpec((t_s, hg * dv), lambda b, h: (rb0 + b, h)),
                   pl.BlockSpec((None, hg, dk, dv), lambda b, h: (b, h, 0, 0))),
        input_output_aliases={10: 0},
        compiler_params=_params(("parallel", "parallel"), 16 << 20),
        name="retention_sample",
    )(block_decay, z, z, z, z, decay, xi, zeta, gain, state_in, rr)
    return rr, st_p, st_s


def _merge_kernel(*refs, parts):
    (oa_ref, rr_ref, wa_ref, wr_ref, ga_ref, gr_ref), (o_ref,), cast_src, cast_dst, _ = _split_refs(refs, 6, 1)
    for rows in _row_parts(o_ref, parts):
        branch_a = jnp.dot(oa_ref[rows, :], wa_ref[...], preferred_element_type=F32)
        branch_r = jnp.dot(rr_ref[rows, :], wr_ref[...], preferred_element_type=F32)
        merged = jax.nn.sigmoid(ga_ref[rows, :]) * branch_a + jax.nn.sigmoid(gr_ref[rows, :]) * branch_r
        o_ref[rows, :] = merged.astype(o_ref.dtype)
    _run_side_casts(cast_src, cast_dst)


def merge_branches(o_a, rr, w_a, w_r, z, cfg, cast_jobs=()):
    rows, ka = o_a.shape
    kr = rr.shape[1]
    d = w_a.shape[1]
    tm = _pick(rows, 768, 8)
    tn = _pick(d, 512, LANES)
    ni, nj = rows // tm, d // tn
    casts = [SideCast(st, layer, ni * nj, nj) for st, layer in cast_jobs]
    ga0, gr0 = cfg["off_gate_a"] // tn, cfg["off_gate_r"] // tn
    assert cfg["off_gate_a"] % tn == 0 and cfg["off_gate_r"] % tn == 0
    need = (2 * (tm * (ka + kr) * 2 + (ka + kr) * tn * 2 + 2 * tm * tn * 4 + tm * tn * 2) + 4 * tm * tn * 4
            + sum(c.vmem for c in casts))
    out = pl.pallas_call(
        functools.partial(_merge_kernel, parts=2 if tm % 16 == 0 else 1),
        out_shape=[jax.ShapeDtypeStruct((rows, d), BF16)] + [c.out_shape for c in casts],
        grid=(ni, nj),
        in_specs=[pl.BlockSpec((tm, ka), lambda i, j: (i, 0)),
                  pl.BlockSpec((tm, kr), lambda i, j: (i, 0)),
                  pl.BlockSpec((ka, tn), lambda i, j: (0, j)),
                  pl.BlockSpec((kr, tn), lambda i, j: (0, j)),
                  pl.BlockSpec((tm, tn), lambda i, j: (i, ga0 + j)),
                  pl.BlockSpec((tm, tn), lambda i, j: (i, gr0 + j))] + [c.in_spec for c in casts],
        out_specs=[pl.BlockSpec((tm, tn), lambda i, j: (i, j))] + [c.out_spec for c in casts],
        compiler_params=_params(("arbitrary", "arbitrary"), need),
        name="merge_branches",
    )(o_a, rr, w_a, w_r, z, z, *[c.operand for c in casts])
    return out[0], out[1:]


def _matmul_residual_kernel(*refs, parts, fold_norm):
    ins, outs, cast_src, cast_dst, _ = _split_refs(refs, 4 if fold_norm else 3, 3 if fold_norm else 1)
    a_ref, w_ref, x_ref = ins[:3]
    o_ref = outs[0]
    if fold_norm:
        g_ref, (xg_ref, ssq_ref) = ins[3], outs[1:]

        @pl.when(pl.program_id(1) == 0)
        def _():
            ssq_ref[...] = jnp.zeros(ssq_ref.shape, F32)

    for rows in _row_parts(o_ref, parts):
        y = x_ref[rows, :] + jnp.dot(a_ref[rows, :], w_ref[...], preferred_element_type=F32)
        o_ref[rows, :] = y
        if fold_norm:
            xg_ref[rows, :] = (y * g_ref[...]).astype(BF16)
            ssq_ref[rows, :] += _lane_group_sum(y * y)
    _run_side_casts(cast_src, cast_dst)


def matmul_residual(a, w, x, tm_target, tn_target, next_gain=None, cast_jobs=()):
    rows, k = a.shape
    n = w.shape[1]
    tm = _pick(rows, tm_target, 8)
    tn = _pick(n, tn_target, LANES)
    ni, nj = rows // tm, n // tn
    fold_norm = next_gain is not None
    casts = [SideCast(st, layer, ni * nj, nj) for st, layer in cast_jobs]
    need = (2 * (tm * k * 2 + k * tn * 2 + 2 * tm * tn * 4 + tm * tn * 2 + tm * LANES * 4) + 3 * tm * tn * 4
            + sum(c.vmem for c in casts))
    tile = pl.BlockSpec((tm, tn), lambda i, j: (i, j))
    in_specs = [pl.BlockSpec((tm, k), lambda i, j: (i, 0)), pl.BlockSpec((k, tn), lambda i, j: (0, j)), tile]
    out_specs, out_shape, operands = [tile], [jax.ShapeDtypeStruct((rows, n), F32)], [a, w, x]
    if fold_norm:
        in_specs.append(pl.BlockSpec((1, tn), lambda i, j: (0, j)))
        operands.append(next_gain.reshape(1, n))
        out_specs += [tile, pl.BlockSpec((tm, LANES), lambda i, j: (i, 0))]
        out_shape += [jax.ShapeDtypeStruct((rows, n), BF16), jax.ShapeDtypeStruct((rows, LANES), F32)]
    out = pl.pallas_call(
        functools.partial(_matmul_residual_kernel, parts=4 if tm % 64 == 0 else 1, fold_norm=fold_norm),
        out_shape=out_shape + [c.out_shape for c in casts],
        grid=(ni, nj),
        in_specs=in_specs + [c.in_spec for c in casts],
        out_specs=out_specs + [c.out_spec for c in casts],
        compiler_params=_params(("arbitrary", "arbitrary"), need),
        name="matmul_residual",
    )(*operands, *[c.operand for c in casts])
    n_own = len(out_shape)
    return out[:n_own], out[n_own:]


def _gate_up_kernel(*refs, parts, d_model):
    (xg_ref, wg_ref, wu_ref, ssq_ref), (o_ref,), cast_src, cast_dst, (rinv_ref,) = _split_refs(refs, 4, 1, 1)
    _store_row_scale(ssq_ref, rinv_ref, d_model)
    for rows in _row_parts(o_ref, parts):
        xg = xg_ref[rows, :]
        rinv = rinv_ref[rows, :]
        gate_acc = jnp.dot(xg, wg_ref[...], preferred_element_type=F32)
        up_acc = jnp.dot(xg, wu_ref[...], preferred_element_type=F32)
        for g in range(o_ref.shape[1] // LANES):
            cols = slice(g * LANES, (g + 1) * LANES)
            gate = gate_acc[:, cols] * rinv
            up = up_acc[:, cols] * rinv
            o_ref[rows, cols] = ((gate * jax.nn.sigmoid(gate)) * up).astype(o_ref.dtype)
    _run_side_casts(cast_src, cast_dst)


def gate_up(xg, ssq, w_gate_up, cast_jobs=()):
    rows, d = xg.shape
    f = w_gate_up.shape[1] // 2
    tm = _pick(rows, 1536, 8)
    tn = _pick(f, 256, LANES)
    ni, nj = rows // tm, f // tn
    casts = [SideCast(st, layer, ni * nj, nj) for st, layer in cast_jobs]
    need = (2 * (tm * d * 2 + 2 * d * tn * 2 + tm * tn * 2 + tm * LANES * 4) + 4 * tm * tn * 4 + tm * LANES * 4
            + sum(c.vmem for c in casts))
    out = pl.pallas_call(
        functools.partial(_gate_up_kernel, parts=2 if tm % 16 == 0 else 1, d_model=d),
        out_shape=[jax.ShapeDtypeStruct((rows, f), BF16)] + [c.out_shape for c in casts],
        grid=(ni, nj),
        in_specs=[pl.BlockSpec((tm, d), lambda i, j: (i, 0)),
                  pl.BlockSpec((d, tn), lambda i, j: (0, j)),
                  pl.BlockSpec((d, tn), lambda i, j: (0, nj + j)),
                  pl.BlockSpec((tm, LANES), lambda i, j: (i, 0))] + [c.in_spec for c in casts],
        out_specs=[pl.BlockSpec((tm, tn), lambda i, j: (i, j))] + [c.out_spec for c in casts],
        scratch_shapes=[pltpu.VMEM((tm, LANES), F32)],
        compiler_params=_params(("arbitrary", "arbitrary"), need),
        name="swiglu_gate_up",
    )(xg, w_gate_up, w_gate_up, ssq, *[c.operand for c in casts])
    return out[0], out[1:]


def _rope_tables(positions, hd):
    half = hd // 2
    inv = ROPE_THETA ** (-jnp.arange(half, dtype=F32) / half)
    ang = positions.astype(F32)[:, None] * inv[None, :]
    cos, sin = jnp.cos(ang), jnp.sin(ang)
    return jnp.concatenate([cos, cos], axis=-1), jnp.concatenate([-sin, sin], axis=-1)


def _retention_tables(heads, length):
    lg = jnp.log1p(-jnp.exp2(-5.0 - jnp.arange(heads, dtype=F32)))
    j = jnp.arange(length, dtype=F32)
    rel = j[:, None] - j[None, :]
    decay = jnp.where(rel >= 0, jnp.exp(jnp.maximum(rel, 0.0)[None] * lg[:, None, None]), 0.0)
    xi = jnp.exp((j + 1.0)[None, :] * lg[:, None])[:, :, None]
    zeta = jnp.exp((length - 1.0 - j)[None, :] * lg[:, None])[:, :, None]
    block_decay = jnp.exp(length * lg)
    return decay, xi, zeta, block_decay


def kernel(x_prompt, x_sample, cache_swa_k, cache_swa_v, state_retention, w_in, w_proj_a, w_proj_r, w_out,
           attn_sinks, ret_norm_gain, norm_mix, norm_ffn, w_gate_up, w_down, norm_final):
    b_p, t_p, d = x_prompt.shape
    b_s, t_s, _ = x_sample.shape
    depth = w_in.shape[0]
    _, _, window, kvh, hd = cache_swa_k.shape
    att_heads = attn_sinks.shape[1]
    _, _, ret_heads, ret_dk, ret_dv = state_retention.shape
    att_w, kv_w = att_heads * hd, kvh * hd
    rqk_w, rv_w = ret_heads * ret_dk, ret_heads * ret_dv
    splits = (att_w, kv_w, kv_w, rqk_w, rqk_w, rv_w, rv_w, d, d)
    offs = [0]
    for s in splits:
        offs.append(offs[-1] + s)
    assert offs[-1] == w_in.shape[2] and hd == LANES and ret_dk == LANES
    cfg = dict(b_p=b_p, t_p=t_p, b_s=b_s, t_s=t_s, head_dim=hd, kv_heads=kvh, att_group=att_heads // kvh,
               att_width=att_w, ret_heads=ret_heads, ret_dk=ret_dk, ret_dv=ret_dv,
               off_k_a=offs[1], off_v_a=offs[2], off_q_r=offs[3], off_k_r=offs[4], off_v_r=offs[5],
               off_g_r=offs[6], off_gate_a=offs[7], off_gate_r=offs[8])

    tn_in = _pick(kv_w, 512, LANES)
    assert all(o % tn_in == 0 for o in offs)
    kind_of_split = (KIND_ROPE_SCALED, KIND_ROPE, KIND_PLAIN, KIND_ROPE, KIND_ROPE_SCALED,
                     KIND_PLAIN, KIND_PLAIN, KIND_PLAIN, KIND_PLAIN)
    assert hd == ret_dk
    kinds = jnp.asarray([kd for kd, s in zip(kind_of_split, splits) for _ in range(s // tn_in)], jnp.int32)

    pos = jnp.concatenate([jnp.tile(jnp.arange(t_p, dtype=jnp.int32), b_p),
                           jnp.tile(PAST_LEN + jnp.arange(t_s, dtype=jnp.int32), b_s)])
    cos_t, sin_t = _rope_tables(pos, hd)
    ret_consts_p = _retention_tables(ret_heads, _pick(t_p, RET_BLOCK_TARGET, CHUNK))
    ret_consts_s = _retention_tables(ret_heads, t_s)

    cache_k = cache_swa_k.reshape(depth, b_s, window, kv_w)
    cache_v = cache_swa_v.reshape(depth, b_s, window, kv_w)
    w_in_b, w_a_b, w_r_b, w_out_b, w_down_b = (w[0].astype(BF16) for w in (w_in, w_proj_a, w_proj_r, w_out, w_down))

    rows_p = b_p * t_p
    x, xg, ssq = norm_prep(x_prompt.reshape(rows_p, d), x_sample.reshape(b_s * t_s, d), norm_mix[0])

    kp_rows, vp_rows, sp_states, ks_rows, vs_rows, ss_states = [], [], [], [], [], []
    for layer in range(depth):
        nxt = layer + 1

        def next_layer(*stacked):
            return [(w, nxt) for w in stacked] if nxt < depth else []

        z, (w_gu_b,) = inproj(xg, ssq, w_in_b, kinds, cos_t, sin_t, tn_in, hd ** -0.5, [(w_gate_up, layer)])
        o_a, (kc_p, vc_p, kn_s, vn_s) = swa(z, attn_sinks[layer], cache_k, cache_v, layer, cfg)
        rr, st_p, st_s = retention(z, state_retention, ret_norm_gain[layer], layer, ret_consts_p, ret_consts_s, cfg)
        merged, next_proj = merge_branches(o_a, rr, w_a_b, w_r_b, z, cfg, next_layer(w_proj_a, w_proj_r))
        (x, xg, ssq), next_out = matmul_residual(merged, w_out_b, x, 768, 512, norm_ffn[layer], next_layer(w_out))
        act, next_in = gate_up(xg, ssq, w_gu_b, next_layer(w_in))
        if nxt < depth:
            (x, xg, ssq), next_down = matmul_residual(act, w_down_b, x, 768, 256, norm_mix[nxt], next_layer(w_down))
            (w_a_b, w_r_b), (w_out_b,), (w_in_b,), (w_down_b,) = next_proj, next_out, next_in, next_down
        else:
            (x,), _ = matmul_residual(act, w_down_b, x, 768, 256)

        kp_rows.append(kc_p.reshape(b_p, window, kvh, hd))
        vp_rows.append(vc_p.reshape(b_p, window, kvh, hd))
        ks_rows.append(kn_s.reshape(b_s, t_s, kvh, hd))
        vs_rows.append(vn_s.reshape(b_s, t_s, kvh, hd))
        sp_states.append(st_p)
        ss_states.append(st_s)

    y_p, y_s = rmsnorm_split(x, norm_final, rows_p)
    return (y_p.reshape(b_p, t_p, d), y_s.reshape(b_s, t_s, d),
            jnp.stack(kp_rows), jnp.stack(vp_rows), jnp.stack(sp_states),
            jnp.stack(ks_rows), jnp.stack(vs_rows), jnp.stack(ss_states))
```

```python
import functools
import math

import jax
import jax.numpy as jnp
from jax import lax
from jax.experimental import pallas as pl
from jax.experimental.pallas import tpu as pltpu

F32 = jnp.float32
BF16 = jnp.bfloat16

CHUNK = 64
PAST_LEN = 2048
ROPE_THETA = 10000.0
RMS_EPS = 1e-6
GN_EPS = 1e-5

LANES = 128
SUBLANES_BF16 = 16
VMEM_BYTES_V7X = 64 * 1024 * 1024
VMEM_BUDGET = VMEM_BYTES_V7X - 8 * 1024 * 1024

RET_BLOCK_TARGET = 256


def _pick(total, target, mult):
    best = None
    for d in range(mult, min(total, target) + 1, mult):
        if total % d == 0:
            best = d
    if best is None:
        raise ValueError(f"no tile for {total} (target {target}, multiple of {mult})")
    return best


def _params(semantics, vmem_need):
    limit = min(max(int(vmem_need * 1.2), 16 * 1024 * 1024), VMEM_BUDGET)
    return pltpu.CompilerParams(dimension_semantics=semantics, vmem_limit_bytes=limit)


def _row_parts(ref, parts):
    rows = ref.shape[0]
    sub = rows // parts
    return [slice(p * sub, (p + 1) * sub) for p in range(parts)]


class SideCast:
    def __init__(self, stacked, layer, n_steps, nj):
        _, k, n = stacked.shape
        rows = min(r for r in range(SUBLANES_BF16, k + 1, SUBLANES_BF16) if k % r == 0 and k // r <= n_steps)
        last = k // rows - 1
        self.operand = stacked
        self.out_shape = jax.ShapeDtypeStruct((k, n), BF16)
        self.in_spec = pl.BlockSpec((None, rows, n), lambda i, j, *_: (layer, jnp.minimum(i * nj + j, last), 0))
        self.out_spec = pl.BlockSpec((rows, n), lambda i, j, *_: (jnp.minimum(i * nj + j, last), 0))
        self.vmem = 2 * rows * n * 6


def _run_side_casts(src_refs, dst_refs):
    for src, dst in zip(src_refs, dst_refs):
        dst[...] = src[...].astype(BF16)


def _split_refs(refs, n_in, n_out, n_scratch=0):
    n_cast = (len(refs) - n_in - n_out - n_scratch) // 2
    a, b, c = n_in + n_cast, n_in + n_cast + n_out, n_in + 2 * n_cast + n_out
    return refs[:n_in], refs[a:b], refs[n_in:a], refs[b:c], refs[c:]


def _lane_group_sum(x):
    acc = x[:, :LANES]
    for t in range(1, x.shape[1] // LANES):
        acc = acc + x[:, t * LANES:(t + 1) * LANES]
    return acc


def _store_row_scale(ssq_ref, scale_ref, d_model):
    @pl.when(pl.program_id(1) == 0)
    def _():
        mean_sq = jnp.sum(ssq_ref[...], axis=-1, keepdims=True) / d_model
        scale_ref[...] = jnp.broadcast_to(lax.rsqrt(mean_sq + RMS_EPS), scale_ref.shape)


def _norm_prep_kernel(xp_ref, xs_ref, g_ref, x_ref, xg_ref, ssq_ref, *, head_blocks):
    def emit(src_ref):
        x = src_ref[...]
        x_ref[...] = x
        xg_ref[...] = (x * g_ref[...]).astype(BF16)
        ssq_ref[...] = _lane_group_sum(x * x)

    i = pl.program_id(0)
    pl.when(i < head_blocks)(functools.partial(emit, xp_ref))
    pl.when(i >= head_blocks)(functools.partial(emit, xs_ref))


def norm_prep(x_head, x_tail, g):
    head_rows, d = x_head.shape
    tail_rows = x_tail.shape[0]
    rows = head_rows + tail_rows
    tr = _pick(math.gcd(head_rows, tail_rows), 256, 8)
    hb = head_rows // tr
    return pl.pallas_call(
        functools.partial(_norm_prep_kernel, head_blocks=hb),
        out_shape=(jax.ShapeDtypeStruct((rows, d), F32), jax.ShapeDtypeStruct((rows, d), BF16),
                   jax.ShapeDtypeStruct((rows, LANES), F32)),
        grid=(rows // tr,),
        in_specs=[pl.BlockSpec((tr, d), lambda i: (jnp.minimum(i, hb - 1), 0)),
                  pl.BlockSpec((tr, d), lambda i: (jnp.maximum(i - hb, 0), 0)),
                  pl.BlockSpec((1, d), lambda i: (0, 0))],
        out_specs=(pl.BlockSpec((tr, d), lambda i: (i, 0)),
                   pl.BlockSpec((tr, d), lambda i: (i, 0)),
                   pl.BlockSpec((tr, LANES), lambda i: (i, 0))),
        compiler_params=_params(("arbitrary",), 16 * tr * d * 4),
        name="norm_prep",
    )(x_head, x_tail, g.reshape(1, d))


def _rmsnorm_split_kernel(x_ref, g_ref, head_ref, tail_ref, *, head_blocks):
    x = x_ref[...]
    y = x * lax.rsqrt(jnp.mean(x * x, axis=-1, keepdims=True) + RMS_EPS) * g_ref[...]
    i = pl.program_id(0)

    @pl.when(i < head_blocks)
    def _():
        head_ref[...] = y

    @pl.when(i >= head_blocks)
    def _():
        tail_ref[...] = y


def rmsnorm_split(x, g, head_rows):
    rows, d = x.shape
    assert 0 < head_rows < rows
    tr = _pick(math.gcd(head_rows, rows - head_rows), 256, 8)
    hb = head_rows // tr
    return pl.pallas_call(
        functools.partial(_rmsnorm_split_kernel, head_blocks=hb),
        out_shape=(jax.ShapeDtypeStruct((head_rows, d), F32), jax.ShapeDtypeStruct((rows - head_rows, d), F32)),
        grid=(rows // tr,),
        in_specs=[pl.BlockSpec((tr, d), lambda i: (i, 0)),
                  pl.BlockSpec((1, d), lambda i: (0, 0))],
        out_specs=(pl.BlockSpec((tr, d), lambda i: (jnp.minimum(i, hb - 1), 0)),
                   pl.BlockSpec((tr, d), lambda i: (jnp.maximum(i - hb, 0), 0))),
        compiler_params=_params(("arbitrary",), 10 * tr * d * 4),
        name="rmsnorm_split",
    )(x, g.reshape(1, d))


KIND_PLAIN, KIND_ROPE, KIND_ROPE_SCALED, KIND_PLAIN_BF16 = 0, 1, 2, 3


def _inproj_kernel(kind_ref, *refs, scale, parts, d_model):
    ((xg_ref, w_ref, cos_ref, sin_ref, ssq_ref), (o_ref, ob_ref), cast_src, cast_dst,
     (rinv_ref,)) = _split_refs(refs, 5, 2, 1)
    kind = kind_ref[pl.program_id(1)]
    _store_row_scale(ssq_ref, rinv_ref, d_model)
    groups = [slice(g * LANES, (g + 1) * LANES) for g in range(o_ref.shape[1] // LANES)]

    def plain(dst_ref):
        for rows in _row_parts(dst_ref, parts):
            acc = jnp.dot(xg_ref[rows, :], w_ref[...], preferred_element_type=F32)
            rinv = rinv_ref[rows, :]
            for cols in groups:
                dst_ref[rows, cols] = (acc[:, cols] * rinv).astype(dst_ref.dtype)
        _run_side_casts(cast_src, cast_dst)

    pl.when(kind == KIND_PLAIN)(functools.partial(plain, o_ref))
    pl.when(kind == KIND_PLAIN_BF16)(functools.partial(plain, ob_ref))

    @pl.when((kind == KIND_ROPE) | (kind == KIND_ROPE_SCALED))
    def _():
        s = jnp.where(kind == KIND_ROPE_SCALED, F32(scale), F32(1.0))
        for rows in _row_parts(o_ref, parts):
            acc = jnp.dot(xg_ref[rows, :], w_ref[...], preferred_element_type=F32)
            rinv = rinv_ref[rows, :]
            c = cos_ref[rows, :]
            sn = sin_ref[rows, :]
            for cols in groups:
                xn = acc[:, cols] * rinv
                r = xn * c + pltpu.roll(xn, LANES // 2, 1) * sn
                o_ref[rows, cols] = r * s
        _run_side_casts(cast_src, cast_dst)


def inproj(xg, ssq, w, kinds, n_f32, cos_t, sin_t, tn, scale, cast_jobs=()):
    rows, d = xg.shape
    n = w.shape[1]
    tm = _pick(rows, 1536, 8)
    ni, nj = rows // tm, n // tn
    f32_tiles = n_f32 // tn
    casts = [SideCast(st, layer, ni * nj, nj) for st, layer in cast_jobs]
    need = (2 * (tm * d * 2 + d * tn * 2 + tm * tn * 6 + 3 * tm * LANES * 4) + 5 * tm * LANES * 4
            + sum(c.vmem for c in casts))
    grid_spec = pltpu.PrefetchScalarGridSpec(
        num_scalar_prefetch=1,
        grid=(ni, nj),
        in_specs=[pl.BlockSpec((tm, d), lambda i, j, kr: (i, 0)),
                  pl.BlockSpec((d, tn), lambda i, j, kr: (0, j)),
                  pl.BlockSpec((tm, LANES), lambda i, j, kr: (i, 0)),
                  pl.BlockSpec((tm, LANES), lambda i, j, kr: (i, 0)),
                  pl.BlockSpec((tm, LANES), lambda i, j, kr: (i, 0))] + [c.in_spec for c in casts],
        out_specs=[pl.BlockSpec((tm, tn), lambda i, j, kr: (i, jnp.minimum(j, f32_tiles - 1))),
                   pl.BlockSpec((tm, tn), lambda i, j, kr: (i, jnp.maximum(j - f32_tiles, 0)))]
                  + [c.out_spec for c in casts],
        scratch_shapes=[pltpu.VMEM((tm, LANES), F32)],
    )
    out = pl.pallas_call(
        functools.partial(_inproj_kernel, scale=scale, parts=4 if tm % 64 == 0 else 1, d_model=d),
        out_shape=[jax.ShapeDtypeStruct((rows, n_f32), F32), jax.ShapeDtypeStruct((rows, n - n_f32), BF16)]
                  + [c.out_shape for c in casts],
        grid_spec=grid_spec,
        compiler_params=_params(("arbitrary", "arbitrary"), need),
        name="inproj_rope",
    )(kinds, xg, w, cos_t, sin_t, ssq, *[c.operand for c in casts])
    return out[0], out[1], out[2:]


def _sink_column(sink_ref, kvh, group, rows_per_head):
    cols = [jnp.full((rows_per_head, 1), sink_ref[kvh * group + g], F32) for g in range(group)]
    return jnp.concatenate(cols, axis=0)


def _swa_chunk(q_rows, kb, vb, sink_col, group):
    hd = kb.shape[1]
    q = jnp.concatenate([q_rows[:, g * hd:(g + 1) * hd] for g in range(group)], axis=0).astype(BF16)
    s = lax.dot_general(q, kb, (((1,), (1,)), ((), ())), preferred_element_type=F32)
    m = jnp.maximum(jnp.max(s, axis=-1, keepdims=True), sink_col)
    p = jnp.exp(s - m)
    denom = jnp.sum(p, axis=-1, keepdims=True) + jnp.exp(sink_col - m)
    p = (p / denom).astype(BF16)
    return jnp.dot(p, vb, preferred_element_type=F32)


def _swa_prompt_kernel(sink_ref, q_ref, k_ref, v_ref, o_ref, kc_ref, vc_ref, kb_ref, vb_ref, s_ref, p_ref, *,
                       group, window_chunks, pad_chunks, unroll):
    seq, hd = k_ref.shape
    n_chunks = seq // CHUNK
    kvh = pl.program_id(1)
    front = (window_chunks + pad_chunks) * CHUNK
    band = front + CHUNK
    kb_ref[:front, :] = jnp.zeros((front, hd), BF16)
    vb_ref[:front, :] = jnp.zeros((front, hd), BF16)
    kb_ref[front:, :] = k_ref[...].astype(BF16)
    vb_ref[front:, :] = v_ref[...].astype(BF16)
    kc_ref[...] = k_ref[seq - kc_ref.shape[0]:, :]
    vc_ref[...] = v_ref[seq - vc_ref.shape[0]:, :]

    def chunk_rows(c):
        return pl.ds(pl.multiple_of(c * CHUNK, CHUNK), CHUNK)

    def band_rows(c):
        return pl.ds(pl.multiple_of(c * CHUNK, CHUNK), band)

    def scores(c, carry):
        q_rows = q_ref[chunk_rows(c), :]
        q = jnp.concatenate([q_rows[:, g * hd:(g + 1) * hd] for g in range(group)], axis=0).astype(BF16)
        s_ref[c] = lax.dot_general(q, kb_ref[band_rows(c), :], (((1,), (1,)), ((), ())),
                                   preferred_element_type=F32)
        return carry

    band_col = lax.broadcasted_iota(jnp.int32, (CHUNK, band), 1)

    def softmax(c, carry):
        outside = band_col < jnp.maximum(pad_chunks * CHUNK, front - c * CHUNK)
        for g in range(group):
            rows = slice(g * CHUNK, (g + 1) * CHUNK)
            sink = sink_ref[kvh * group + g]
            s = jnp.where(outside, F32(-1e30), s_ref[c, rows, :])
            m = jnp.maximum(jnp.max(s, axis=-1, keepdims=True), sink)
            p = jnp.exp(s - m)
            denom = jnp.sum(p, axis=-1, keepdims=True) + jnp.exp(sink - m)
            p_ref[c, rows, :] = (p / denom).astype(BF16)
        return carry

    def values(c, carry):
        o = jnp.dot(p_ref[c], vb_ref[band_rows(c), :], preferred_element_type=F32)
        for g in range(group):
            o_ref[chunk_rows(c), g * hd:(g + 1) * hd] = o[g * CHUNK:(g + 1) * CHUNK].astype(o_ref.dtype)
        return carry

    lax.fori_loop(0, n_chunks, scores, 0, unroll=2 * unroll)
    lax.fori_loop(0, n_chunks, softmax, 0, unroll=unroll)
    lax.fori_loop(0, n_chunks, values, 0, unroll=2 * unroll)


def _swa_sample_kernel(sink_ref, q_ref, k_ref, v_ref, ck_ref, cv_ref, o_in_ref, o_ref, kn_ref, vn_ref, *, group):
    del o_in_ref
    hd = k_ref.shape[1]
    kn_ref[...] = k_ref[...]
    vn_ref[...] = v_ref[...]
    kb = jnp.concatenate([ck_ref[...], k_ref[...]], axis=0).astype(BF16)
    vb = jnp.concatenate([cv_ref[...], v_ref[...]], axis=0).astype(BF16)
    sink_col = _sink_column(sink_ref, pl.program_id(1), group, q_ref.shape[0])
    o = _swa_chunk(q_ref[...], kb, vb, sink_col, group)
    rows = q_ref.shape[0]
    for g in range(group):
        o_ref[:, g * hd:(g + 1) * hd] = o[g * rows:(g + 1) * rows].astype(o_ref.dtype)


def swa(z, sinks, cache_k, cache_v, layer, cfg):
    hd, kvh, group = cfg["head_dim"], cfg["kv_heads"], cfg["att_group"]
    b_p, t_p, b_s, t_s = cfg["b_p"], cfg["t_p"], cfg["b_s"], cfg["t_s"]
    rows = z.shape[0]
    qw = group * hd
    k_blk0 = cfg["off_k_a"] // hd
    v_blk0 = cfg["off_v_a"] // hd
    att_w = cfg["att_width"]
    window = cache_k.shape[2]
    window_chunks = window // CHUNK
    smem = pl.BlockSpec(memory_space=pltpu.SMEM)

    lane_chunks = LANES // CHUNK
    pad_chunks = -(window_chunks + 1) % lane_chunks
    band = (window_chunks + pad_chunks + 1) * CHUNK
    front = band - CHUNK
    n_chunks = t_p // CHUNK
    need_p = (2 * (t_p * qw * 4 + 2 * t_p * hd * 4 + t_p * qw * 2) + 2 * (front + t_p) * hd * 2
              + n_chunks * group * CHUNK * band * 6 + (4 << 20))
    kv_w = kvh * hd
    o, kc_p, vc_p = pl.pallas_call(
        functools.partial(_swa_prompt_kernel, group=group, window_chunks=window_chunks, pad_chunks=pad_chunks,
                          unroll=2),
        out_shape=(jax.ShapeDtypeStruct((rows, att_w), BF16),
                   jax.ShapeDtypeStruct((b_p, window, kv_w), F32),
                   jax.ShapeDtypeStruct((b_p, window, kv_w), F32)),
        grid=(b_p, kvh),
        in_specs=[smem,
                  pl.BlockSpec((t_p, qw), lambda b, k: (b, k)),
                  pl.BlockSpec((t_p, hd), lambda b, k: (b, k_blk0 + k)),
                  pl.BlockSpec((t_p, hd), lambda b, k: (b, v_blk0 + k))],
        out_specs=(pl.BlockSpec((t_p, qw), lambda b, k: (b, k)),
                   pl.BlockSpec((None, window, hd), lambda b, k: (b, 0, k)),
                   pl.BlockSpec((None, window, hd), lambda b, k: (b, 0, k))),
        scratch_shapes=[pltpu.VMEM((front + t_p, hd), BF16), pltpu.VMEM((front + t_p, hd), BF16),
                        pltpu.VMEM((n_chunks, group * CHUNK, band), F32),
                        pltpu.VMEM((n_chunks, group * CHUNK, band), BF16)],
        compiler_params=_params(("parallel", "parallel"), need_p),
        name="swa_prompt",
    )(sinks, z, z, z)

    assert t_s == CHUNK and window + t_s == (window_chunks + 1) * CHUNK
    rb0 = (b_p * t_p) // t_s
    o, kn_s, vn_s = pl.pallas_call(
        functools.partial(_swa_sample_kernel, group=group),
        out_shape=(jax.ShapeDtypeStruct((rows, att_w), BF16),
                   jax.ShapeDtypeStruct((b_s, t_s, kv_w), F32),
                   jax.ShapeDtypeStruct((b_s, t_s, kv_w), F32)),
        grid=(b_s, kvh),
        in_specs=[smem,
                  pl.BlockSpec((t_s, qw), lambda b, k: (rb0 + b, k)),
                  pl.BlockSpec((t_s, hd), lambda b, k: (rb0 + b, k_blk0 + k)),
                  pl.BlockSpec((t_s, hd), lambda b, k: (rb0 + b, v_blk0 + k)),
                  pl.BlockSpec((None, None, window, hd), lambda b, k: (layer, b, 0, k)),
                  pl.BlockSpec((None, None, window, hd), lambda b, k: (layer, b, 0, k)),
                  pl.BlockSpec(memory_space=pl.ANY)],
        out_specs=(pl.BlockSpec((t_s, qw), lambda b, k: (rb0 + b, k)),
                   pl.BlockSpec((None, t_s, hd), lambda b, k: (b, 0, k)),
                   pl.BlockSpec((None, t_s, hd), lambda b, k: (b, 0, k))),
        input_output_aliases={6: 0},
        compiler_params=_params(("parallel", "parallel"), 8 << 20),
        name="swa_sample",
    )(sinks, z, z, z, cache_k, cache_v, o)
    return o, (kc_p, vc_p, kn_s, vn_s)


def _retention_state_update(k32, v_bf, zeta):
    kz_t = jnp.transpose(k32 * zeta).astype(BF16)
    return jnp.dot(kz_t, v_bf, preferred_element_type=F32)


def _retention_readout(q_bf, k_bf, v_bf, g32, state_bf, decay, xi, gain):
    s = lax.dot_general(q_bf, k_bf, (((1,), (1,)), ((), ())), preferred_element_type=F32) * decay
    o = (jnp.dot(s.astype(BF16), v_bf, preferred_element_type=F32)
         + jnp.dot(q_bf, state_bf, preferred_element_type=F32) * xi)
    mu = jnp.mean(o, axis=-1, keepdims=True)
    var = jnp.mean(jnp.square(o - mu), axis=-1, keepdims=True)
    on = (o - mu) * lax.rsqrt(var + GN_EPS) * gain
    return (g32 * jax.nn.sigmoid(g32)) * on


def _retention_prompt_kernel(cd_ref, q_ref, k_ref, v_ref, g_ref, decay_ref, xi_ref, zeta_ref, gain_ref,
                             o_ref, st_ref, sb_ref, *, block, unroll):
    h = pl.program_id(1)
    n_blocks = q_ref.shape[0] // block
    block_decay = cd_ref[h]
    zeta = zeta_ref[...]

    state = jnp.zeros(st_ref.shape, F32)
    for c in range(n_blocks):
        rows = slice(c * block, (c + 1) * block)
        sb_ref[c] = state.astype(BF16)
        state = block_decay * state + _retention_state_update(k_ref[rows, :], v_ref[rows, :].astype(BF16), zeta)
    st_ref[...] = state

    gain = gain_ref[pl.ds(h, 1), :]

    def body(t, carry):
        for u in range(unroll):
            c = t * unroll + u
            rows = pl.ds(pl.multiple_of(c * block, block), block)
            out = _retention_readout(q_ref[rows, :].astype(BF16), k_ref[rows, :].astype(BF16),
                                     v_ref[rows, :].astype(BF16), g_ref[rows, :].astype(F32), sb_ref[c],
                                     decay_ref[...], xi_ref[...], gain)
            o_ref[rows, :] = out.astype(o_ref.dtype)
        return carry

    lax.fori_loop(0, n_blocks // unroll, body, 0)


def _retention_sample_kernel(cd_ref, q_ref, k_ref, v_ref, g_ref, decay_ref, xi_ref, zeta_ref, gain_ref,
                             st_in_ref, o_in_ref, o_ref, st_ref):
    del o_in_ref
    n_heads, dk, dv = st_in_ref.shape
    head0 = pl.program_id(1) * n_heads
    for i in range(n_heads):
        qk_cols, v_cols = slice(i * dk, (i + 1) * dk), slice(i * dv, (i + 1) * dv)
        state = st_in_ref[i]
        k32 = k_ref[:, qk_cols]
        v_bf = v_ref[:, v_cols].astype(BF16)
        out = _retention_readout(q_ref[:, qk_cols].astype(BF16), k32.astype(BF16), v_bf, g_ref[:, v_cols].astype(F32),
                                 state.astype(BF16), decay_ref[i], xi_ref[i], gain_ref[pl.ds(head0 + i, 1), :])
        o_ref[:, v_cols] = out.astype(o_ref.dtype)
        st_ref[i] = cd_ref[head0 + i] * state + _retention_state_update(k32, v_bf, zeta_ref[i])


def retention(z32, zb, state_in, gain, layer, consts_p, consts_s, cfg):
    dk, dv, heads = cfg["ret_dk"], cfg["ret_dv"], cfg["ret_heads"]
    b_p, t_p, b_s, t_s = cfg["b_p"], cfg["t_p"], cfg["b_s"], cfg["t_s"]
    rows = z32.shape[0]
    smem = pl.BlockSpec(memory_space=pltpu.SMEM)
    col_offsets = (("off_q_r", dk), ("off_k_r", dk), ("off_v_rb", dv), ("off_g_rb", dv))

    def specs(t, length, hg, row_block0):
        lead, hg = hg, hg or 1
        col0 = [cfg[o] // (hg * w) for o, w in col_offsets]
        return ([pl.BlockSpec((t, hg * w), lambda b, h, c0=c0: (row_block0 + b, c0 + h))
                 for c0, (_, w) in zip(col0, col_offsets)] +
                [pl.BlockSpec((lead, length, length), lambda b, h: (h, 0, 0)),
                 pl.BlockSpec((lead, length, 1), lambda b, h: (h, 0, 0)),
                 pl.BlockSpec((lead, length, 1), lambda b, h: (h, 0, 0)),
                 pl.BlockSpec((heads, dv), lambda b, h: (0, 0))])

    decay, xi, zeta, block_decay = consts_p
    block = decay.shape[1]
    n_blocks = t_p // block
    unroll = max(u for u in (1, 2, 4, 8) if n_blocks % u == 0)
    need_p = (2 * (2 * t_p * dk * 4 + 2 * t_p * dv * 4 + t_p * dv * 2 + block * block * 4)
              + n_blocks * dk * dv * 2 + (8 << 20))
    rr, st_p = pl.pallas_call(
        functools.partial(_retention_prompt_kernel, block=block, unroll=unroll),
        out_shape=(jax.ShapeDtypeStruct((rows, heads * dv), BF16),
                   jax.ShapeDtypeStruct((b_p, heads, dk, dv), F32)),
        grid=(b_p, heads),
        in_specs=[smem] + specs(t_p, block, None, 0),
        out_specs=(pl.BlockSpec((t_p, dv), lambda b, h: (b, h)),
                   pl.BlockSpec((None, None, dk, dv), lambda b, h: (b, h, 0, 0))),
        scratch_shapes=[pltpu.VMEM((n_blocks, dk, dv), BF16)],
        compiler_params=_params(("parallel", "parallel"), need_p),
        name="retention_prompt",
    )(block_decay, z32, z32, zb, zb, decay, xi, zeta, gain)

    decay, xi, zeta, block_decay = consts_s
    assert decay.shape[1] == t_s
    rb0 = (b_p * t_p) // t_s
    hg = max(n for n in range(1, heads + 1)
             if heads % n == 0 and all(cfg[o] % (n * w) == 0 for o, w in col_offsets))
    rr, st_s = pl.pallas_call(
        _retention_sample_kernel,
        out_shape=(jax.ShapeDtypeStruct((rows, heads * dv), BF16),
                   jax.ShapeDtypeStruct((b_s, heads, dk, dv), F32)),
        grid=(b_s, heads // hg),
        in_specs=[smem] + specs(t_s, t_s, hg, rb0) +
                 [pl.BlockSpec((None, None, hg, dk, dv), lambda b, h: (layer, b, h, 0, 0)),
                  pl.BlockSpec(memory_space=pl.ANY)],
        out_specs=(pl.BlockSpec((t_s, hg * dv), lambda b, h: (rb0 + b, h)),
                   pl.BlockSpec((None, hg, dk, dv), lambda b, h: (b, h, 0, 0))),
        input_output_aliases={10: 0},
        compiler_params=_params(("parallel", "parallel"), 16 << 20),
        name="retention_sample",
    )(block_decay, z32, z32, zb, zb, decay, xi, zeta, gain, state_in, rr)
    return rr, st_p, st_s


def _merge_kernel(*refs, parts):
    (oa_ref, rr_ref, wa_ref, wr_ref, ga_ref, gr_ref), (o_ref,), cast_src, cast_dst, _ = _split_refs(refs, 6, 1)
    for rows in _row_parts(o_ref, parts):
        branch_a = jnp.dot(oa_ref[rows, :], wa_ref[...], preferred_element_type=F32)
        branch_r = jnp.dot(rr_ref[rows, :], wr_ref[...], preferred_element_type=F32)
        merged = (jax.nn.sigmoid(ga_ref[rows, :].astype(F32)) * branch_a
                  + jax.nn.sigmoid(gr_ref[rows, :].astype(F32)) * branch_r)
        o_ref[rows, :] = merged.astype(o_ref.dtype)
    _run_side_casts(cast_src, cast_dst)


def merge_branches(o_a, rr, w_a, w_r, zb, cfg, cast_jobs=()):
    rows, ka = o_a.shape
    kr = rr.shape[1]
    d = w_a.shape[1]
    tm = _pick(rows, 768, 8)
    tn = _pick(d, 512, LANES)
    ni, nj = rows // tm, d // tn
    casts = [SideCast(st, layer, ni * nj, nj) for st, layer in cast_jobs]
    ga0, gr0 = cfg["off_gate_ab"] // tn, cfg["off_gate_rb"] // tn
    assert cfg["off_gate_ab"] % tn == 0 and cfg["off_gate_rb"] % tn == 0
    need = (2 * (tm * (ka + kr) * 2 + (ka + kr) * tn * 2 + 2 * tm * tn * 4 + tm * tn * 2) + 4 * tm * tn * 4
            + sum(c.vmem for c in casts))
    out = pl.pallas_call(
        functools.partial(_merge_kernel, parts=2 if tm % 16 == 0 else 1),
        out_shape=[jax.ShapeDtypeStruct((rows, d), BF16)] + [c.out_shape for c in casts],
        grid=(ni, nj),
        in_specs=[pl.BlockSpec((tm, ka), lambda i, j: (i, 0)),
                  pl.BlockSpec((tm, kr), lambda i, j: (i, 0)),
                  pl.BlockSpec((ka, tn), lambda i, j: (0, j)),
                  pl.BlockSpec((kr, tn), lambda i, j: (0, j)),
                  pl.BlockSpec((tm, tn), lambda i, j: (i, ga0 + j)),
                  pl.BlockSpec((tm, tn), lambda i, j: (i, gr0 + j))] + [c.in_spec for c in casts],
        out_specs=[pl.BlockSpec((tm, tn), lambda i, j: (i, j))] + [c.out_spec for c in casts],
        compiler_params=_params(("arbitrary", "arbitrary"), need),
        name="merge_branches",
    )(o_a, rr, w_a, w_r, zb, zb, *[c.operand for c in casts])
    return out[0], out[1:]


def _matmul_residual_kernel(*refs, parts, fold_norm):
    ins, outs, cast_src, cast_dst, _ = _split_refs(refs, 4 if fold_norm else 3, 3 if fold_norm else 1)
    a_ref, w_ref, x_ref = ins[:3]
    o_ref = outs[0]
    if fold_norm:
        g_ref, (xg_ref, ssq_ref) = ins[3], outs[1:]

        @pl.when(pl.program_id(1) == 0)
        def _():
            ssq_ref[...] = jnp.zeros(ssq_ref.shape, F32)

    for rows in _row_parts(o_ref, parts):
        y = x_ref[rows, :] + jnp.dot(a_ref[rows, :], w_ref[...], preferred_element_type=F32)
        o_ref[rows, :] = y
        if fold_norm:
            xg_ref[rows, :] = (y * g_ref[...]).astype(BF16)
            ssq_ref[rows, :] += _lane_group_sum(y * y)
    _run_side_casts(cast_src, cast_dst)


def matmul_residual(a, w, x, tm_target, tn_target, next_gain=None, cast_jobs=()):
    rows, k = a.shape
    n = w.shape[1]
    tm = _pick(rows, tm_target, 8)
    tn = _pick(n, tn_target, LANES)
    ni, nj = rows // tm, n // tn
    fold_norm = next_gain is not None
    casts = [SideCast(st, layer, ni * nj, nj) for st, layer in cast_jobs]
    need = (2 * (tm * k * 2 + k * tn * 2 + 2 * tm * tn * 4 + tm * tn * 2 + tm * LANES * 4) + 3 * tm * tn * 4
            + sum(c.vmem for c in casts))
    tile = pl.BlockSpec((tm, tn), lambda i, j: (i, j))
    in_specs = [pl.BlockSpec((tm, k), lambda i, j: (i, 0)), pl.BlockSpec((k, tn), lambda i, j: (0, j)), tile]
    out_specs, out_shape, operands = [tile], [jax.ShapeDtypeStruct((rows, n), F32)], [a, w, x]
    if fold_norm:
        in_specs.append(pl.BlockSpec((1, tn), lambda i, j: (0, j)))
        operands.append(next_gain.reshape(1, n))
        out_specs += [tile, pl.BlockSpec((tm, LANES), lambda i, j: (i, 0))]
        out_shape += [jax.ShapeDtypeStruct((rows, n), BF16), jax.ShapeDtypeStruct((rows, LANES), F32)]
    out = pl.pallas_call(
        functools.partial(_matmul_residual_kernel, parts=4 if tm % 64 == 0 else 1, fold_norm=fold_norm),
        out_shape=out_shape + [c.out_shape for c in casts],
        grid=(ni, nj),
        in_specs=in_specs + [c.in_spec for c in casts],
        out_specs=out_specs + [c.out_spec for c in casts],
        compiler_params=_params(("arbitrary", "arbitrary"), need),
        name="matmul_residual",
    )(*operands, *[c.operand for c in casts])
    n_own = len(out_shape)
    return out[:n_own], out[n_own:]


def _gate_up_kernel(*refs, parts, d_model):
    (xg_ref, wg_ref, wu_ref, ssq_ref), (o_ref,), cast_src, cast_dst, (rinv_ref,) = _split_refs(refs, 4, 1, 1)
    _store_row_scale(ssq_ref, rinv_ref, d_model)
    for rows in _row_parts(o_ref, parts):
        xg = xg_ref[rows, :]
        rinv = rinv_ref[rows, :]
        gate_acc = jnp.dot(xg, wg_ref[...], preferred_element_type=F32)
        up_acc = jnp.dot(xg, wu_ref[...], preferred_element_type=F32)
        for g in range(o_ref.shape[1] // LANES):
            cols = slice(g * LANES, (g + 1) * LANES)
            gate = gate_acc[:, cols] * rinv
            up = up_acc[:, cols] * rinv
            o_ref[rows, cols] = ((gate * jax.nn.sigmoid(gate)) * up).astype(o_ref.dtype)
    _run_side_casts(cast_src, cast_dst)


def gate_up(xg, ssq, w_gate_up, cast_jobs=()):
    rows, d = xg.shape
    f = w_gate_up.shape[1] // 2
    tm = _pick(rows, 1536, 8)
    tn = _pick(f, 256, LANES)
    ni, nj = rows // tm, f // tn
    casts = [SideCast(st, layer, ni * nj, nj) for st, layer in cast_jobs]
    need = (2 * (tm * d * 2 + 2 * d * tn * 2 + tm * tn * 2 + tm * LANES * 4) + 4 * tm * tn * 4 + tm * LANES * 4
            + sum(c.vmem for c in casts))
    out = pl.pallas_call(
        functools.partial(_gate_up_kernel, parts=2 if tm % 16 == 0 else 1, d_model=d),
        out_shape=[jax.ShapeDtypeStruct((rows, f), BF16)] + [c.out_shape for c in casts],
        grid=(ni, nj),
        in_specs=[pl.BlockSpec((tm, d), lambda i, j: (i, 0)),
                  pl.BlockSpec((d, tn), lambda i, j: (0, j)),
                  pl.BlockSpec((d, tn), lambda i, j: (0, nj + j)),
                  pl.BlockSpec((tm, LANES), lambda i, j: (i, 0))] + [c.in_spec for c in casts],
        out_specs=[pl.BlockSpec((tm, tn), lambda i, j: (i, j))] + [c.out_spec for c in casts],
        scratch_shapes=[pltpu.VMEM((tm, LANES), F32)],
        compiler_params=_params(("arbitrary", "arbitrary"), need),
        name="swiglu_gate_up",
    )(xg, w_gate_up, w_gate_up, ssq, *[c.operand for c in casts])
    return out[0], out[1:]


def _rope_tables(positions, hd):
    half = hd // 2
    inv = ROPE_THETA ** (-jnp.arange(half, dtype=F32) / half)
    ang = positions.astype(F32)[:, None] * inv[None, :]
    cos, sin = jnp.cos(ang), jnp.sin(ang)
    return jnp.concatenate([cos, cos], axis=-1), jnp.concatenate([-sin, sin], axis=-1)


def _retention_tables(heads, length):
    lg = jnp.log1p(-jnp.exp2(-5.0 - jnp.arange(heads, dtype=F32)))
    j = jnp.arange(length, dtype=F32)
    rel = j[:, None] - j[None, :]
    decay = jnp.where(rel >= 0, jnp.exp(jnp.maximum(rel, 0.0)[None] * lg[:, None, None]), 0.0)
    xi = jnp.exp((j + 1.0)[None, :] * lg[:, None])[:, :, None]
    zeta = jnp.exp((length - 1.0 - j)[None, :] * lg[:, None])[:, :, None]
    block_decay = jnp.exp(length * lg)
    return decay, xi, zeta, block_decay


def kernel(x_prompt, x_sample, cache_swa_k, cache_swa_v, state_retention, w_in, w_proj_a, w_proj_r, w_out,
           attn_sinks, ret_norm_gain, norm_mix, norm_ffn, w_gate_up, w_down, norm_final):
    b_p, t_p, d = x_prompt.shape
    b_s, t_s, _ = x_sample.shape
    depth = w_in.shape[0]
    _, _, window, kvh, hd = cache_swa_k.shape
    att_heads = attn_sinks.shape[1]
    _, _, ret_heads, ret_dk, ret_dv = state_retention.shape
    att_w, kv_w = att_heads * hd, kvh * hd
    rqk_w, rv_w = ret_heads * ret_dk, ret_heads * ret_dv
    splits = (att_w, kv_w, kv_w, rqk_w, rqk_w, rv_w, rv_w, d, d)
    offs = [0]
    for s in splits:
        offs.append(offs[-1] + s)
    assert offs[-1] == w_in.shape[2] and hd == LANES and ret_dk == LANES
    cfg = dict(b_p=b_p, t_p=t_p, b_s=b_s, t_s=t_s, head_dim=hd, kv_heads=kvh, att_group=att_heads // kvh,
               att_width=att_w, ret_heads=ret_heads, ret_dk=ret_dk, ret_dv=ret_dv,
               off_k_a=offs[1], off_v_a=offs[2], off_q_r=offs[3], off_k_r=offs[4],
               off_v_rb=0, off_g_rb=offs[6] - offs[5], off_gate_ab=offs[7] - offs[5], off_gate_rb=offs[8] - offs[5])
    n_f32 = offs[5]

    tn_in = _pick(kv_w, 512, LANES)
    assert all(o % tn_in == 0 for o in offs)
    kind_of_split = (KIND_ROPE_SCALED, KIND_ROPE, KIND_PLAIN, KIND_ROPE, KIND_ROPE_SCALED,
                     KIND_PLAIN_BF16, KIND_PLAIN_BF16, KIND_PLAIN_BF16, KIND_PLAIN_BF16)
    assert hd == ret_dk
    kinds = jnp.asarray([kd for kd, s in zip(kind_of_split, splits) for _ in range(s // tn_in)], jnp.int32)

    pos = jnp.concatenate([jnp.tile(jnp.arange(t_p, dtype=jnp.int32), b_p),
                           jnp.tile(PAST_LEN + jnp.arange(t_s, dtype=jnp.int32), b_s)])
    cos_t, sin_t = _rope_tables(pos, hd)
    ret_consts_p = _retention_tables(ret_heads, _pick(t_p, RET_BLOCK_TARGET, CHUNK))
    ret_consts_s = _retention_tables(ret_heads, t_s)

    cache_k = cache_swa_k.reshape(depth, b_s, window, kv_w)
    cache_v = cache_swa_v.reshape(depth, b_s, window, kv_w)
    w_in_b, w_a_b, w_r_b, w_out_b, w_down_b = (w[0].astype(BF16) for w in (w_in, w_proj_a, w_proj_r, w_out, w_down))

    rows_p = b_p * t_p
    x, xg, ssq = norm_prep(x_prompt.reshape(rows_p, d), x_sample.reshape(b_s * t_s, d), norm_mix[0])

    kp_rows, vp_rows, sp_states, ks_rows, vs_rows, ss_states = [], [], [], [], [], []
    for layer in range(depth):
        nxt = layer + 1

        def next_layer(*stacked):
            return [(w, nxt) for w in stacked] if nxt < depth else []

        z32, zb, (w_gu_b,) = inproj(xg, ssq, w_in_b, kinds, n_f32, cos_t, sin_t, tn_in, hd ** -0.5,
                                    [(w_gate_up, layer)])
        o_a, (kc_p, vc_p, kn_s, vn_s) = swa(z32, attn_sinks[layer], cache_k, cache_v, layer, cfg)
        rr, st_p, st_s = retention(z32, zb, state_retention, ret_norm_gain[layer], layer,
                                   ret_consts_p, ret_consts_s, cfg)
        merged, next_proj = merge_branches(o_a, rr, w_a_b, w_r_b, zb, cfg, next_layer(w_proj_a, w_proj_r))
        (x, xg, ssq), next_out = matmul_residual(merged, w_out_b, x, 768, 512, norm_ffn[layer], next_layer(w_out))
        act, next_in = gate_up(xg, ssq, w_gu_b, next_layer(w_in))
        if nxt < depth:
            (x, xg, ssq), next_down = matmul_residual(act, w_down_b, x, 768, 256, norm_mix[nxt], next_layer(w_down))
            (w_a_b, w_r_b), (w_out_b,), (w_in_b,), (w_down_b,) = next_proj, next_out, next_in, next_down
        else:
            (x,), _ = matmul_residual(act, w_down_b, x, 768, 256)

        kp_rows.append(kc_p.reshape(b_p, window, kvh, hd))
        vp_rows.append(vc_p.reshape(b_p, window, kvh, hd))
        ks_rows.append(kn_s.reshape(b_s, t_s, kvh, hd))
        vs_rows.append(vn_s.reshape(b_s, t_s, kvh, hd))
        sp_states.append(st_p)
        ss_states.append(st_s)

    y_p, y_s = rmsnorm_split(x, norm_final, rows_p)
    return (y_p.reshape(b_p, t_p, d), y_s.reshape(b_s, t_s, d),
            jnp.stack(kp_rows), jnp.stack(vp_rows), jnp.stack(sp_states),
            jnp.stack(ks_rows), jnp.stack(vs_rows), jnp.stack(ss_states))
```

```python
import functools
import math

import jax
import jax.numpy as jnp
from jax import lax
from jax.experimental import pallas as pl
from jax.experimental.pallas import tpu as pltpu

F32 = jnp.float32
BF16 = jnp.bfloat16

CHUNK = 64
PAST_LEN = 2048
ROPE_THETA = 10000.0
RMS_EPS = 1e-6
GN_EPS = 1e-5

LANES = 128
SUBLANES_BF16 = 16
VMEM_BYTES_V7X = 64 * 1024 * 1024
VMEM_BUDGET = VMEM_BYTES_V7X - 8 * 1024 * 1024

RET_BLOCK_TARGET = 256


def _pick(total, target, mult):
    best = None
    for d in range(mult, min(total, target) + 1, mult):
        if total % d == 0:
            best = d
    if best is None:
        raise ValueError(f"no tile for {total} (target {target}, multiple of {mult})")
    return best


def _params(semantics, vmem_need):
    limit = min(max(int(vmem_need * 1.2), 16 * 1024 * 1024), VMEM_BUDGET)
    return pltpu.CompilerParams(dimension_semantics=semantics, vmem_limit_bytes=limit)


def _row_parts(ref, parts):
    rows = ref.shape[0]
    sub = rows // parts
    return [slice(p * sub, (p + 1) * sub) for p in range(parts)]


class SideCast:
    def __init__(self, stacked, layer, n_steps, nj):
        _, k, n = stacked.shape
        rows = min(r for r in range(SUBLANES_BF16, k + 1, SUBLANES_BF16) if k % r == 0 and k // r <= n_steps)
        last = k // rows - 1
        self.operand = stacked
        self.out_shape = jax.ShapeDtypeStruct((k, n), BF16)
        self.in_spec = pl.BlockSpec((None, rows, n), lambda i, j, *_: (layer, jnp.minimum(i * nj + j, last), 0))
        self.out_spec = pl.BlockSpec((rows, n), lambda i, j, *_: (jnp.minimum(i * nj + j, last), 0))
        self.vmem = 2 * rows * n * 6


def _run_side_casts(src_refs, dst_refs):
    for src, dst in zip(src_refs, dst_refs):
        dst[...] = src[...].astype(BF16)


def _split_refs(refs, n_in, n_out, n_scratch=0):
    n_cast = (len(refs) - n_in - n_out - n_scratch) // 2
    a, b, c = n_in + n_cast, n_in + n_cast + n_out, n_in + 2 * n_cast + n_out
    return refs[:n_in], refs[a:b], refs[n_in:a], refs[b:c], refs[c:]


def _lane_group_sum(x):
    acc = x[:, :LANES]
    for t in range(1, x.shape[1] // LANES):
        acc = acc + x[:, t * LANES:(t + 1) * LANES]
    return acc


def _store_row_scale(ssq_ref, scale_ref, d_model):
    @pl.when(pl.program_id(1) == 0)
    def _():
        mean_sq = jnp.sum(ssq_ref[...], axis=-1, keepdims=True) / d_model
        scale_ref[...] = jnp.broadcast_to(lax.rsqrt(mean_sq + RMS_EPS), scale_ref.shape)


def _norm_prep_kernel(xp_ref, xs_ref, g_ref, x_ref, xg_ref, ssq_ref, *, head_blocks):
    def emit(src_ref):
        x = src_ref[...]
        x_ref[...] = x
        xg_ref[...] = (x * g_ref[...]).astype(BF16)
        ssq_ref[...] = _lane_group_sum(x * x)

    i = pl.program_id(0)
    pl.when(i < head_blocks)(functools.partial(emit, xp_ref))
    pl.when(i >= head_blocks)(functools.partial(emit, xs_ref))


def norm_prep(x_head, x_tail, g):
    head_rows, d = x_head.shape
    tail_rows = x_tail.shape[0]
    rows = head_rows + tail_rows
    tr = _pick(math.gcd(head_rows, tail_rows), 256, 8)
    hb = head_rows // tr
    return pl.pallas_call(
        functools.partial(_norm_prep_kernel, head_blocks=hb),
        out_shape=(jax.ShapeDtypeStruct((rows, d), F32), jax.ShapeDtypeStruct((rows, d), BF16),
                   jax.ShapeDtypeStruct((rows, LANES), F32)),
        grid=(rows // tr,),
        in_specs=[pl.BlockSpec((tr, d), lambda i: (jnp.minimum(i, hb - 1), 0)),
                  pl.BlockSpec((tr, d), lambda i: (jnp.maximum(i - hb, 0), 0)),
                  pl.BlockSpec((1, d), lambda i: (0, 0))],
        out_specs=(pl.BlockSpec((tr, d), lambda i: (i, 0)),
                   pl.BlockSpec((tr, d), lambda i: (i, 0)),
                   pl.BlockSpec((tr, LANES), lambda i: (i, 0))),
        compiler_params=_params(("arbitrary",), 16 * tr * d * 4),
        name="norm_prep",
    )(x_head, x_tail, g.reshape(1, d))


def _rmsnorm_split_kernel(x_ref, g_ref, head_ref, tail_ref, *, head_blocks):
    x = x_ref[...]
    y = x * lax.rsqrt(jnp.mean(x * x, axis=-1, keepdims=True) + RMS_EPS) * g_ref[...]
    i = pl.program_id(0)

    @pl.when(i < head_blocks)
    def _():
        head_ref[...] = y

    @pl.when(i >= head_blocks)
    def _():
        tail_ref[...] = y


def rmsnorm_split(x, g, head_rows):
    rows, d = x.shape
    assert 0 < head_rows < rows
    tr = _pick(math.gcd(head_rows, rows - head_rows), 256, 8)
    hb = head_rows // tr
    return pl.pallas_call(
        functools.partial(_rmsnorm_split_kernel, head_blocks=hb),
        out_shape=(jax.ShapeDtypeStruct((head_rows, d), F32), jax.ShapeDtypeStruct((rows - head_rows, d), F32)),
        grid=(rows // tr,),
        in_specs=[pl.BlockSpec((tr, d), lambda i: (i, 0)),
                  pl.BlockSpec((1, d), lambda i: (0, 0))],
        out_specs=(pl.BlockSpec((tr, d), lambda i: (jnp.minimum(i, hb - 1), 0)),
                   pl.BlockSpec((tr, d), lambda i: (jnp.maximum(i - hb, 0), 0))),
        compiler_params=_params(("arbitrary",), 10 * tr * d * 4),
        name="rmsnorm_split",
    )(x, g.reshape(1, d))


KIND_PLAIN, KIND_ROPE, KIND_ROPE_SCALED, KIND_PLAIN_BF16 = 0, 1, 2, 3


def _inproj_kernel(kind_ref, *refs, scale, parts, d_model):
    ((xg_ref, w_ref, cos_ref, sin_ref, ssq_ref), (o_ref, ob_ref), cast_src, cast_dst,
     (rinv_ref,)) = _split_refs(refs, 5, 2, 1)
    kind = kind_ref[pl.program_id(1)]
    _store_row_scale(ssq_ref, rinv_ref, d_model)
    groups = [slice(g * LANES, (g + 1) * LANES) for g in range(o_ref.shape[1] // LANES)]

    def plain(dst_ref):
        for rows in _row_parts(dst_ref, parts):
            acc = jnp.dot(xg_ref[rows, :], w_ref[...], preferred_element_type=F32)
            rinv = rinv_ref[rows, :]
            for cols in groups:
                dst_ref[rows, cols] = (acc[:, cols] * rinv).astype(dst_ref.dtype)
        _run_side_casts(cast_src, cast_dst)

    pl.when(kind == KIND_PLAIN)(functools.partial(plain, o_ref))
    pl.when(kind == KIND_PLAIN_BF16)(functools.partial(plain, ob_ref))

    @pl.when((kind == KIND_ROPE) | (kind == KIND_ROPE_SCALED))
    def _():
        s = jnp.where(kind == KIND_ROPE_SCALED, F32(scale), F32(1.0))
        for rows in _row_parts(o_ref, parts):
            acc = jnp.dot(xg_ref[rows, :], w_ref[...], preferred_element_type=F32)
            rinv = rinv_ref[rows, :]
            c = cos_ref[rows, :]
            sn = sin_ref[rows, :]
            for cols in groups:
                xn = acc[:, cols] * rinv
                r = xn * c + pltpu.roll(xn, LANES // 2, 1) * sn
                o_ref[rows, cols] = r * s
        _run_side_casts(cast_src, cast_dst)


def inproj(xg, ssq, w, kinds, n_f32, cos_t, sin_t, tn, scale, cast_jobs=()):
    rows, d = xg.shape
    n = w.shape[1]
    tm = _pick(rows, 1536, 8)
    ni, nj = rows // tm, n // tn
    f32_tiles = n_f32 // tn
    casts = [SideCast(st, layer, ni * nj, nj) for st, layer in cast_jobs]
    need = (2 * (tm * d * 2 + d * tn * 2 + tm * tn * 6 + 3 * tm * LANES * 4) + 5 * tm * LANES * 4
            + sum(c.vmem for c in casts))
    grid_spec = pltpu.PrefetchScalarGridSpec(
        num_scalar_prefetch=1,
        grid=(ni, nj),
        in_specs=[pl.BlockSpec((tm, d), lambda i, j, kr: (i, 0)),
                  pl.BlockSpec((d, tn), lambda i, j, kr: (0, j)),
                  pl.BlockSpec((tm, LANES), lambda i, j, kr: (i, 0)),
                  pl.BlockSpec((tm, LANES), lambda i, j, kr: (i, 0)),
                  pl.BlockSpec((tm, LANES), lambda i, j, kr: (i, 0))] + [c.in_spec for c in casts],
        out_specs=[pl.BlockSpec((tm, tn), lambda i, j, kr: (i, jnp.minimum(j, f32_tiles - 1))),
                   pl.BlockSpec((tm, tn), lambda i, j, kr: (i, jnp.maximum(j - f32_tiles, 0)))]
                  + [c.out_spec for c in casts],
        scratch_shapes=[pltpu.VMEM((tm, LANES), F32)],
    )
    out = pl.pallas_call(
        functools.partial(_inproj_kernel, scale=scale, parts=4 if tm % 64 == 0 else 1, d_model=d),
        out_shape=[jax.ShapeDtypeStruct((rows, n_f32), F32), jax.ShapeDtypeStruct((rows, n - n_f32), BF16)]
                  + [c.out_shape for c in casts],
        grid_spec=grid_spec,
        compiler_params=_params(("arbitrary", "arbitrary"), need),
        name="inproj_rope",
    )(kinds, xg, w, cos_t, sin_t, ssq, *[c.operand for c in casts])
    return out[0], out[1], out[2:]


def _sink_column(sink_ref, kvh, group, rows_per_head):
    cols = [jnp.full((rows_per_head, 1), sink_ref[kvh * group + g], F32) for g in range(group)]
    return jnp.concatenate(cols, axis=0)


def _swa_chunk(q_rows, kb, vb, sink_col, group):
    hd = kb.shape[1]
    q = jnp.concatenate([q_rows[:, g * hd:(g + 1) * hd] for g in range(group)], axis=0).astype(BF16)
    s = lax.dot_general(q, kb, (((1,), (1,)), ((), ())), preferred_element_type=F32)
    m = jnp.maximum(jnp.max(s, axis=-1, keepdims=True), sink_col)
    p = jnp.exp(s - m)
    denom = jnp.sum(p, axis=-1, keepdims=True) + jnp.exp(sink_col - m)
    p = (p / denom).astype(BF16)
    return jnp.dot(p, vb, preferred_element_type=F32)


def _swa_prompt_kernel(sink_ref, q_ref, k_ref, v_ref, o_ref, kc_ref, vc_ref, kb_ref, vb_ref, s_ref, p_ref, *,
                       group, window_chunks, pad_chunks, unroll):
    seq, hd = k_ref.shape
    n_chunks = seq // CHUNK
    kvh = pl.program_id(1)
    front = (window_chunks + pad_chunks) * CHUNK
    band = front + CHUNK
    kb_ref[:front, :] = jnp.zeros((front, hd), BF16)
    vb_ref[:front, :] = jnp.zeros((front, hd), BF16)
    kb_ref[front:, :] = k_ref[...].astype(BF16)
    vb_ref[front:, :] = v_ref[...].astype(BF16)
    kc_ref[...] = k_ref[seq - kc_ref.shape[0]:, :]
    vc_ref[...] = v_ref[seq - vc_ref.shape[0]:, :]

    def chunk_rows(c):
        return pl.ds(pl.multiple_of(c * CHUNK, CHUNK), CHUNK)

    def band_rows(c):
        return pl.ds(pl.multiple_of(c * CHUNK, CHUNK), band)

    def scores(c, carry):
        q_rows = q_ref[chunk_rows(c), :]
        q = jnp.concatenate([q_rows[:, g * hd:(g + 1) * hd] for g in range(group)], axis=0).astype(BF16)
        s_ref[c] = lax.dot_general(q, kb_ref[band_rows(c), :], (((1,), (1,)), ((), ())),
                                   preferred_element_type=F32)
        return carry

    band_col = lax.broadcasted_iota(jnp.int32, (CHUNK, band), 1)

    def softmax(c, carry):
        outside = band_col < jnp.maximum(pad_chunks * CHUNK, front - c * CHUNK)
        for g in range(group):
            rows = slice(g * CHUNK, (g + 1) * CHUNK)
            sink = sink_ref[kvh * group + g]
            s = jnp.where(outside, F32(-1e30), s_ref[c, rows, :])
            m = jnp.maximum(jnp.max(s, axis=-1, keepdims=True), sink)
            p = jnp.exp(s - m)
            denom = jnp.sum(p, axis=-1, keepdims=True) + jnp.exp(sink - m)
            p_ref[c, rows, :] = (p / denom).astype(BF16)
        return carry

    def values(c, carry):
        o = jnp.dot(p_ref[c], vb_ref[band_rows(c), :], preferred_element_type=F32)
        for g in range(group):
            o_ref[chunk_rows(c), g * hd:(g + 1) * hd] = o[g * CHUNK:(g + 1) * CHUNK].astype(o_ref.dtype)
        return carry

    lax.fori_loop(0, n_chunks, scores, 0, unroll=2 * unroll)
    lax.fori_loop(0, n_chunks, softmax, 0, unroll=unroll)
    lax.fori_loop(0, n_chunks, values, 0, unroll=2 * unroll)


def _swa_sample_kernel(sink_ref, q_ref, k_ref, v_ref, ck_ref, cv_ref, o_in_ref, o_ref, kn_ref, vn_ref, *, group):
    del o_in_ref
    hd = k_ref.shape[1]
    kn_ref[...] = k_ref[...]
    vn_ref[...] = v_ref[...]
    kb = jnp.concatenate([ck_ref[...], k_ref[...]], axis=0).astype(BF16)
    vb = jnp.concatenate([cv_ref[...], v_ref[...]], axis=0).astype(BF16)
    sink_col = _sink_column(sink_ref, pl.program_id(1), group, q_ref.shape[0])
    o = _swa_chunk(q_ref[...], kb, vb, sink_col, group)
    rows = q_ref.shape[0]
    for g in range(group):
        o_ref[:, g * hd:(g + 1) * hd] = o[g * rows:(g + 1) * rows].astype(o_ref.dtype)


def swa(z, sinks, cache_k, cache_v, layer, cfg):
    hd, kvh, group = cfg["head_dim"], cfg["kv_heads"], cfg["att_group"]
    b_p, t_p, b_s, t_s = cfg["b_p"], cfg["t_p"], cfg["b_s"], cfg["t_s"]
    rows = z.shape[0]
    qw = group * hd
    k_blk0 = cfg["off_k_a"] // hd
    v_blk0 = cfg["off_v_a"] // hd
    att_w = cfg["att_width"]
    window = cache_k.shape[2]
    window_chunks = window // CHUNK
    smem = pl.BlockSpec(memory_space=pltpu.SMEM)

    lane_chunks = LANES // CHUNK
    pad_chunks = -(window_chunks + 1) % lane_chunks
    band = (window_chunks + pad_chunks + 1) * CHUNK
    front = band - CHUNK
    n_chunks = t_p // CHUNK
    need_p = (2 * (t_p * qw * 4 + 2 * t_p * hd * 4 + t_p * qw * 2) + 2 * (front + t_p) * hd * 2
              + n_chunks * group * CHUNK * band * 6 + (4 << 20))
    kv_w = kvh * hd
    o, kc_p, vc_p = pl.pallas_call(
        functools.partial(_swa_prompt_kernel, group=group, window_chunks=window_chunks, pad_chunks=pad_chunks,
                          unroll=2),
        out_shape=(jax.ShapeDtypeStruct((rows, att_w), BF16),
                   jax.ShapeDtypeStruct((b_p, window, kv_w), F32),
                   jax.ShapeDtypeStruct((b_p, window, kv_w), F32)),
        grid=(b_p, kvh),
        in_specs=[smem,
                  pl.BlockSpec((t_p, qw), lambda b, k: (b, k)),
                  pl.BlockSpec((t_p, hd), lambda b, k: (b, k_blk0 + k)),
                  pl.BlockSpec((t_p, hd), lambda b, k: (b, v_blk0 + k))],
        out_specs=(pl.BlockSpec((t_p, qw), lambda b, k: (b, k)),
                   pl.BlockSpec((None, window, hd), lambda b, k: (b, 0, k)),
                   pl.BlockSpec((None, window, hd), lambda b, k: (b, 0, k))),
        scratch_shapes=[pltpu.VMEM((front + t_p, hd), BF16), pltpu.VMEM((front + t_p, hd), BF16),
                        pltpu.VMEM((n_chunks, group * CHUNK, band), F32),
                        pltpu.VMEM((n_chunks, group * CHUNK, band), BF16)],
        compiler_params=_params(("parallel", "parallel"), need_p),
        name="swa_prompt",
    )(sinks, z, z, z)

    assert t_s == CHUNK and window + t_s == (window_chunks + 1) * CHUNK
    rb0 = (b_p * t_p) // t_s
    o, kn_s, vn_s = pl.pallas_call(
        functools.partial(_swa_sample_kernel, group=group),
        out_shape=(jax.ShapeDtypeStruct((rows, att_w), BF16),
                   jax.ShapeDtypeStruct((b_s, t_s, kv_w), F32),
                   jax.ShapeDtypeStruct((b_s, t_s, kv_w), F32)),
        grid=(b_s, kvh),
        in_specs=[smem,
                  pl.BlockSpec((t_s, qw), lambda b, k: (rb0 + b, k)),
                  pl.BlockSpec((t_s, hd), lambda b, k: (rb0 + b, k_blk0 + k)),
                  pl.BlockSpec((t_s, hd), lambda b, k: (rb0 + b, v_blk0 + k)),
                  pl.BlockSpec((None, None, window, hd), lambda b, k: (layer, b, 0, k)),
                  pl.BlockSpec((None, None, window, hd), lambda b, k: (layer, b, 0, k)),
                  pl.BlockSpec(memory_space=pl.ANY)],
        out_specs=(pl.BlockSpec((t_s, qw), lambda b, k: (rb0 + b, k)),
                   pl.BlockSpec((None, t_s, hd), lambda b, k: (b, 0, k)),
                   pl.BlockSpec((None, t_s, hd), lambda b, k: (b, 0, k))),
        input_output_aliases={6: 0},
        compiler_params=_params(("parallel", "parallel"), 8 << 20),
        name="swa_sample",
    )(sinks, z, z, z, cache_k, cache_v, o)
    return o, (kc_p, vc_p, kn_s, vn_s)


def _retention_state_update(k32, v_bf, zeta):
    kz_t = jnp.transpose(k32 * zeta).astype(BF16)
    return jnp.dot(kz_t, v_bf, preferred_element_type=F32)


def _retention_readout(q_bf, k_bf, v_bf, g32, state_bf, decay, xi, gain):
    s = lax.dot_general(q_bf, k_bf, (((1,), (1,)), ((), ())), preferred_element_type=F32) * decay
    o = (jnp.dot(s.astype(BF16), v_bf, preferred_element_type=F32)
         + jnp.dot(q_bf, state_bf, preferred_element_type=F32) * xi)
    mu = jnp.mean(o, axis=-1, keepdims=True)
    var = jnp.mean(jnp.square(o - mu), axis=-1, keepdims=True)
    on = (o - mu) * lax.rsqrt(var + GN_EPS) * gain
    return (g32 * jax.nn.sigmoid(g32)) * on


def _retention_prompt_kernel(cd_ref, q_ref, k_ref, v_ref, g_ref, decay_ref, xi_ref, zeta_ref, gain_ref,
                             o_ref, st_ref, sb_ref, *, block, unroll):
    h = pl.program_id(1)
    n_blocks = q_ref.shape[0] // block
    block_decay = cd_ref[h]
    zeta = zeta_ref[...]

    state = jnp.zeros(st_ref.shape, F32)
    for c in range(n_blocks):
        rows = slice(c * block, (c + 1) * block)
        sb_ref[c] = state.astype(BF16)
        state = block_decay * state + _retention_state_update(k_ref[rows, :], v_ref[rows, :].astype(BF16), zeta)
    st_ref[...] = state

    gain = gain_ref[pl.ds(h, 1), :]

    def body(t, carry):
        for u in range(unroll):
            c = t * unroll + u
            rows = pl.ds(pl.multiple_of(c * block, block), block)
            out = _retention_readout(q_ref[rows, :].astype(BF16), k_ref[rows, :].astype(BF16),
                                     v_ref[rows, :].astype(BF16), g_ref[rows, :].astype(F32), sb_ref[c],
                                     decay_ref[...], xi_ref[...], gain)
            o_ref[rows, :] = out.astype(o_ref.dtype)
        return carry

    lax.fori_loop(0, n_blocks // unroll, body, 0)


def _retention_sample_kernel(cd_ref, q_ref, k_ref, v_ref, g_ref, decay_ref, xi_ref, zeta_ref, gain_ref,
                             st_in_ref, o_in_ref, o_ref, st_ref):
    del o_in_ref
    n_heads, dk, dv = st_in_ref.shape
    head0 = pl.program_id(1) * n_heads
    for i in range(n_heads):
        qk_cols, v_cols = slice(i * dk, (i + 1) * dk), slice(i * dv, (i + 1) * dv)
        state = st_in_ref[i]
        k32 = k_ref[:, qk_cols]
        v_bf = v_ref[:, v_cols].astype(BF16)
        out = _retention_readout(q_ref[:, qk_cols].astype(BF16), k32.astype(BF16), v_bf, g_ref[:, v_cols].astype(F32),
                                 state.astype(BF16), decay_ref[i], xi_ref[i], gain_ref[pl.ds(head0 + i, 1), :])
        o_ref[:, v_cols] = out.astype(o_ref.dtype)
        st_ref[i] = cd_ref[head0 + i] * state + _retention_state_update(k32, v_bf, zeta_ref[i])


def retention(z32, zb, state_in, gain, layer, consts_p, consts_s, cfg):
    dk, dv, heads = cfg["ret_dk"], cfg["ret_dv"], cfg["ret_heads"]
    b_p, t_p, b_s, t_s = cfg["b_p"], cfg["t_p"], cfg["b_s"], cfg["t_s"]
    rows = z32.shape[0]
    smem = pl.BlockSpec(memory_space=pltpu.SMEM)
    col_offsets = (("off_q_r", dk), ("off_k_r", dk), ("off_v_rb", dv), ("off_g_rb", dv))

    def specs(t, length, hg, row_block0):
        lead, hg = hg, hg or 1
        col0 = [cfg[o] // (hg * w) for o, w in col_offsets]
        return ([pl.BlockSpec((t, hg * w), lambda b, h, c0=c0: (row_block0 + b, c0 + h))
                 for c0, (_, w) in zip(col0, col_offsets)] +
                [pl.BlockSpec((lead, length, length), lambda b, h: (h, 0, 0)),
                 pl.BlockSpec((lead, length, 1), lambda b, h: (h, 0, 0)),
                 pl.BlockSpec((lead, length, 1), lambda b, h: (h, 0, 0)),
                 pl.BlockSpec((heads, dv), lambda b, h: (0, 0))])

    decay, xi, zeta, block_decay = consts_p
    block = decay.shape[1]
    n_blocks = t_p // block
    unroll = max(u for u in (1, 2, 4, 8) if n_blocks % u == 0)
    need_p = (2 * (2 * t_p * dk * 4 + 2 * t_p * dv * 4 + t_p * dv * 2 + block * block * 4)
              + n_blocks * dk * dv * 2 + (8 << 20))
    rr, st_p = pl.pallas_call(
        functools.partial(_retention_prompt_kernel, block=block, unroll=unroll),
        out_shape=(jax.ShapeDtypeStruct((rows, heads * dv), BF16),
                   jax.ShapeDtypeStruct((b_p, heads, dk, dv), F32)),
        grid=(b_p, heads),
        in_specs=[smem] + specs(t_p, block, None, 0),
        out_specs=(pl.BlockSpec((t_p, dv), lambda b, h: (b, h)),
                   pl.BlockSpec((None, None, dk, dv), lambda b, h: (b, h, 0, 0))),
        scratch_shapes=[pltpu.VMEM((n_blocks, dk, dv), BF16)],
        compiler_params=_params(("parallel", "parallel"), need_p),
        name="retention_prompt",
    )(block_decay, z32, z32, zb, zb, decay, xi, zeta, gain)

    decay, xi, zeta, block_decay = consts_s
    assert decay.shape[1] == t_s
    rb0 = (b_p * t_p) // t_s
    hg = max(n for n in range(1, heads + 1)
             if heads % n == 0 and all(cfg[o] % (n * w) == 0 for o, w in col_offsets))
    rr, st_s = pl.pallas_call(
        _retention_sample_kernel,
        out_shape=(jax.ShapeDtypeStruct((rows, heads * dv), BF16),
                   jax.ShapeDtypeStruct((b_s, heads, dk, dv), F32)),
        grid=(b_s, heads // hg),
        in_specs=[smem] + specs(t_s, t_s, hg, rb0) +
                 [pl.BlockSpec((None, None, hg, dk, dv), lambda b, h: (layer, b, h, 0, 0)),
                  pl.BlockSpec(memory_space=pl.ANY)],
        out_specs=(pl.BlockSpec((t_s, hg * dv), lambda b, h: (rb0 + b, h)),
                   pl.BlockSpec((None, hg, dk, dv), lambda b, h: (b, h, 0, 0))),
        input_output_aliases={10: 0},
        compiler_params=_params(("parallel", "parallel"), 16 << 20),
        name="retention_sample",
    )(block_decay, z32, z32, zb, zb, decay, xi, zeta, gain, state_in, rr)
    return rr, st_p, st_s


def _merge_kernel(*refs, parts):
    (oa_ref, rr_ref, wa_ref, wr_ref, ga_ref, gr_ref), (o_ref,), cast_src, cast_dst, _ = _split_refs(refs, 6, 1)
    for rows in _row_parts(o_ref, parts):
        branch_a = jnp.dot(oa_ref[rows, :], wa_ref[...], preferred_element_type=F32)
        branch_r = jnp.dot(rr_ref[rows, :], wr_ref[...], preferred_element_type=F32)
        merged = (jax.nn.sigmoid(ga_ref[rows, :].astype(F32)) * branch_a
                  + jax.nn.sigmoid(gr_ref[rows, :].astype(F32)) * branch_r)
        o_ref[rows, :] = merged.astype(o_ref.dtype)
    _run_side_casts(cast_src, cast_dst)


def merge_branches(o_a, rr, w_a, w_r, zb, cfg, cast_jobs=()):
    rows, ka = o_a.shape
    kr = rr.shape[1]
    d = w_a.shape[1]
    tm = _pick(rows, 768, 8)
    tn = _pick(d, 512, LANES)
    ni, nj = rows // tm, d // tn
    casts = [SideCast(st, layer, ni * nj, nj) for st, layer in cast_jobs]
    ga0, gr0 = cfg["off_gate_ab"] // tn, cfg["off_gate_rb"] // tn
    assert cfg["off_gate_ab"] % tn == 0 and cfg["off_gate_rb"] % tn == 0
    need = (2 * (tm * (ka + kr) * 2 + (ka + kr) * tn * 2 + 2 * tm * tn * 4 + tm * tn * 2) + 4 * tm * tn * 4
            + sum(c.vmem for c in casts))
    out = pl.pallas_call(
        functools.partial(_merge_kernel, parts=4 if tm % 64 == 0 else 1),
        out_shape=[jax.ShapeDtypeStruct((rows, d), BF16)] + [c.out_shape for c in casts],
        grid=(ni, nj),
        in_specs=[pl.BlockSpec((tm, ka), lambda i, j: (i, 0)),
                  pl.BlockSpec((tm, kr), lambda i, j: (i, 0)),
                  pl.BlockSpec((ka, tn), lambda i, j: (0, j)),
                  pl.BlockSpec((kr, tn), lambda i, j: (0, j)),
                  pl.BlockSpec((tm, tn), lambda i, j: (i, ga0 + j)),
                  pl.BlockSpec((tm, tn), lambda i, j: (i, gr0 + j))] + [c.in_spec for c in casts],
        out_specs=[pl.BlockSpec((tm, tn), lambda i, j: (i, j))] + [c.out_spec for c in casts],
        compiler_params=_params(("arbitrary", "arbitrary"), need),
        name="merge_branches",
    )(o_a, rr, w_a, w_r, zb, zb, *[c.operand for c in casts])
    return out[0], out[1:]


def _matmul_residual_kernel(*refs, parts, fold_norm):
    ins, outs, cast_src, cast_dst, _ = _split_refs(refs, 4 if fold_norm else 3, 3 if fold_norm else 1)
    a_ref, w_ref, x_ref = ins[:3]
    o_ref = outs[0]
    if fold_norm:
        g_ref, (xg_ref, ssq_ref) = ins[3], outs[1:]

        @pl.when(pl.program_id(1) == 0)
        def _():
            ssq_ref[...] = jnp.zeros(ssq_ref.shape, F32)

    for rows in _row_parts(o_ref, parts):
        y = x_ref[rows, :] + jnp.dot(a_ref[rows, :], w_ref[...], preferred_element_type=F32)
        o_ref[rows, :] = y
        if fold_norm:
            xg_ref[rows, :] = (y * g_ref[...]).astype(BF16)
            ssq_ref[rows, :] += _lane_group_sum(y * y)
    _run_side_casts(cast_src, cast_dst)


def matmul_residual(a, w, x, tm_target, tn_target, next_gain=None, cast_jobs=()):
    rows, k = a.shape
    n = w.shape[1]
    tm = _pick(rows, tm_target, 8)
    tn = _pick(n, tn_target, LANES)
    ni, nj = rows // tm, n // tn
    fold_norm = next_gain is not None
    casts = [SideCast(st, layer, ni * nj, nj) for st, layer in cast_jobs]
    need = (2 * (tm * k * 2 + k * tn * 2 + 2 * tm * tn * 4 + tm * tn * 2 + tm * LANES * 4) + 3 * tm * tn * 4
            + sum(c.vmem for c in casts))
    tile = pl.BlockSpec((tm, tn), lambda i, j: (i, j))
    in_specs = [pl.BlockSpec((tm, k), lambda i, j: (i, 0)), pl.BlockSpec((k, tn), lambda i, j: (0, j)), tile]
    out_specs, out_shape, operands = [tile], [jax.ShapeDtypeStruct((rows, n), F32)], [a, w, x]
    if fold_norm:
        in_specs.append(pl.BlockSpec((1, tn), lambda i, j: (0, j)))
        operands.append(next_gain.reshape(1, n))
        out_specs += [tile, pl.BlockSpec((tm, LANES), lambda i, j: (i, 0))]
        out_shape += [jax.ShapeDtypeStruct((rows, n), BF16), jax.ShapeDtypeStruct((rows, LANES), F32)]
    out = pl.pallas_call(
        functools.partial(_matmul_residual_kernel, parts=4 if tm % 64 == 0 else 1, fold_norm=fold_norm),
        out_shape=out_shape + [c.out_shape for c in casts],
        grid=(ni, nj),
        in_specs=in_specs + [c.in_spec for c in casts],
        out_specs=out_specs + [c.out_spec for c in casts],
        compiler_params=_params(("arbitrary", "arbitrary"), need),
        name="matmul_residual",
    )(*operands, *[c.operand for c in casts])
    n_own = len(out_shape)
    return out[:n_own], out[n_own:]


def _gate_up_kernel(*refs, parts, d_model):
    (xg_ref, wg_ref, wu_ref, ssq_ref), (o_ref,), cast_src, cast_dst, (rinv_ref,) = _split_refs(refs, 4, 1, 1)
    _store_row_scale(ssq_ref, rinv_ref, d_model)
    for rows in _row_parts(o_ref, parts):
        xg = xg_ref[rows, :]
        rinv = rinv_ref[rows, :]
        gate_acc = jnp.dot(xg, wg_ref[...], preferred_element_type=F32)
        up_acc = jnp.dot(xg, wu_ref[...], preferred_element_type=F32)
        for g in range(o_ref.shape[1] // LANES):
            cols = slice(g * LANES, (g + 1) * LANES)
            gate = gate_acc[:, cols] * rinv
            up = up_acc[:, cols] * rinv
            o_ref[rows, cols] = ((gate * jax.nn.sigmoid(gate)) * up).astype(o_ref.dtype)
    _run_side_casts(cast_src, cast_dst)


def gate_up(xg, ssq, w_gate_up, cast_jobs=()):
    rows, d = xg.shape
    f = w_gate_up.shape[1] // 2
    tm = _pick(rows, 1536, 8)
    tn = _pick(f, 256, LANES)
    ni, nj = rows // tm, f // tn
    casts = [SideCast(st, layer, ni * nj, nj) for st, layer in cast_jobs]
    need = (2 * (tm * d * 2 + 2 * d * tn * 2 + tm * tn * 2 + tm * LANES * 4) + 4 * tm * tn * 4 + tm * LANES * 4
            + sum(c.vmem for c in casts))
    out = pl.pallas_call(
        functools.partial(_gate_up_kernel, parts=2 if tm % 16 == 0 else 1, d_model=d),
        out_shape=[jax.ShapeDtypeStruct((rows, f), BF16)] + [c.out_shape for c in casts],
        grid=(ni, nj),
        in_specs=[pl.BlockSpec((tm, d), lambda i, j: (i, 0)),
                  pl.BlockSpec((d, tn), lambda i, j: (0, j)),
                  pl.BlockSpec((d, tn), lambda i, j: (0, nj + j)),
                  pl.BlockSpec((tm, LANES), lambda i, j: (i, 0))] + [c.in_spec for c in casts],
        out_specs=[pl.BlockSpec((tm, tn), lambda i, j: (i, j))] + [c.out_spec for c in casts],
        scratch_shapes=[pltpu.VMEM((tm, LANES), F32)],
        compiler_params=_params(("arbitrary", "arbitrary"), need),
        name="swiglu_gate_up",
    )(xg, w_gate_up, w_gate_up, ssq, *[c.operand for c in casts])
    return out[0], out[1:]


def _rope_tables(positions, hd):
    half = hd // 2
    inv = ROPE_THETA ** (-jnp.arange(half, dtype=F32) / half)
    ang = positions.astype(F32)[:, None] * inv[None, :]
    cos, sin = jnp.cos(ang), jnp.sin(ang)
    return jnp.concatenate([cos, cos], axis=-1), jnp.concatenate([-sin, sin], axis=-1)


def _retention_tables(heads, length):
    lg = jnp.log1p(-jnp.exp2(-5.0 - jnp.arange(heads, dtype=F32)))
    j = jnp.arange(length, dtype=F32)
    rel = j[:, None] - j[None, :]
    decay = jnp.where(rel >= 0, jnp.exp(jnp.maximum(rel, 0.0)[None] * lg[:, None, None]), 0.0)
    xi = jnp.exp((j + 1.0)[None, :] * lg[:, None])[:, :, None]
    zeta = jnp.exp((length - 1.0 - j)[None, :] * lg[:, None])[:, :, None]
    block_decay = jnp.exp(length * lg)
    return decay, xi, zeta, block_decay


def kernel(x_prompt, x_sample, cache_swa_k, cache_swa_v, state_retention, w_in, w_proj_a, w_proj_r, w_out,
           attn_sinks, ret_norm_gain, norm_mix, norm_ffn, w_gate_up, w_down, norm_final):
    b_p, t_p, d = x_prompt.shape
    b_s, t_s, _ = x_sample.shape
    depth = w_in.shape[0]
    _, _, window, kvh, hd = cache_swa_k.shape
    att_heads = attn_sinks.shape[1]
    _, _, ret_heads, ret_dk, ret_dv = state_retention.shape
    att_w, kv_w = att_heads * hd, kvh * hd
    rqk_w, rv_w = ret_heads * ret_dk, ret_heads * ret_dv
    splits = (att_w, kv_w, kv_w, rqk_w, rqk_w, rv_w, rv_w, d, d)
    offs = [0]
    for s in splits:
        offs.append(offs[-1] + s)
    assert offs[-1] == w_in.shape[2] and hd == LANES and ret_dk == LANES
    cfg = dict(b_p=b_p, t_p=t_p, b_s=b_s, t_s=t_s, head_dim=hd, kv_heads=kvh, att_group=att_heads // kvh,
               att_width=att_w, ret_heads=ret_heads, ret_dk=ret_dk, ret_dv=ret_dv,
               off_k_a=offs[1], off_v_a=offs[2], off_q_r=offs[3], off_k_r=offs[4],
               off_v_rb=0, off_g_rb=offs[6] - offs[5], off_gate_ab=offs[7] - offs[5], off_gate_rb=offs[8] - offs[5])
    n_f32 = offs[5]

    tn_in = _pick(kv_w, 512, LANES)
    assert all(o % tn_in == 0 for o in offs)
    kind_of_split = (KIND_ROPE_SCALED, KIND_ROPE, KIND_PLAIN, KIND_ROPE, KIND_ROPE_SCALED,
                     KIND_PLAIN_BF16, KIND_PLAIN_BF16, KIND_PLAIN_BF16, KIND_PLAIN_BF16)
    assert hd == ret_dk
    kinds = jnp.asarray([kd for kd, s in zip(kind_of_split, splits) for _ in range(s // tn_in)], jnp.int32)

    pos = jnp.concatenate([jnp.tile(jnp.arange(t_p, dtype=jnp.int32), b_p),
                           jnp.tile(PAST_LEN + jnp.arange(t_s, dtype=jnp.int32), b_s)])
    cos_t, sin_t = _rope_tables(pos, hd)
    ret_consts_p = _retention_tables(ret_heads, _pick(t_p, RET_BLOCK_TARGET, CHUNK))
    ret_consts_s = _retention_tables(ret_heads, t_s)

    cache_k = cache_swa_k.reshape(depth, b_s, window, kv_w)
    cache_v = cache_swa_v.reshape(depth, b_s, window, kv_w)
    w_in_b, w_a_b, w_r_b, w_out_b, w_down_b = (w[0].astype(BF16) for w in (w_in, w_proj_a, w_proj_r, w_out, w_down))

    rows_p = b_p * t_p
    x, xg, ssq = norm_prep(x_prompt.reshape(rows_p, d), x_sample.reshape(b_s * t_s, d), norm_mix[0])

    kp_rows, vp_rows, sp_states, ks_rows, vs_rows, ss_states = [], [], [], [], [], []
    for layer in range(depth):
        nxt = layer + 1

        def next_layer(*stacked):
            return [(w, nxt) for w in stacked] if nxt < depth else []

        z32, zb, (w_gu_b,) = inproj(xg, ssq, w_in_b, kinds, n_f32, cos_t, sin_t, tn_in, hd ** -0.5,
                                    [(w_gate_up, layer)])
        o_a, (kc_p, vc_p, kn_s, vn_s) = swa(z32, attn_sinks[layer], cache_k, cache_v, layer, cfg)
        rr, st_p, st_s = retention(z32, zb, state_retention, ret_norm_gain[layer], layer,
                                   ret_consts_p, ret_consts_s, cfg)
        merged, next_proj = merge_branches(o_a, rr, w_a_b, w_r_b, zb, cfg, next_layer(w_proj_a, w_proj_r))
        (x, xg, ssq), next_out = matmul_residual(merged, w_out_b, x, 768, 1024, norm_ffn[layer], next_layer(w_out))
        act, next_in = gate_up(xg, ssq, w_gu_b, next_layer(w_in))
        if nxt < depth:
            (x, xg, ssq), next_down = matmul_residual(act, w_down_b, x, 768, 256, norm_mix[nxt], next_layer(w_down))
            (w_a_b, w_r_b), (w_out_b,), (w_in_b,), (w_down_b,) = next_proj, next_out, next_in, next_down
        else:
            (x,), _ = matmul_residual(act, w_down_b, x, 768, 256)

        kp_rows.append(kc_p.reshape(b_p, window, kvh, hd))
        vp_rows.append(vc_p.reshape(b_p, window, kvh, hd))
        ks_rows.append(kn_s.reshape(b_s, t_s, kvh, hd))
        vs_rows.append(vn_s.reshape(b_s, t_s, kvh, hd))
        sp_states.append(st_p)
        ss_states.append(st_s)

    y_p, y_s = rmsnorm_split(x, norm_final, rows_p)
    return (y_p.reshape(b_p, t_p, d), y_s.reshape(b_s, t_s, d),
            jnp.stack(kp_rows), jnp.stack(vp_rows), jnp.stack(sp_states),
            jnp.stack(ks_rows), jnp.stack(vs_rows), jnp.stack(ss_states))
```

```python
import functools
import math

import jax
import jax.numpy as jnp
from jax import lax
from jax.experimental import pallas as pl
from jax.experimental.pallas import tpu as pltpu

F32 = jnp.float32
BF16 = jnp.bfloat16

CHUNK = 64
PAST_LEN = 2048
ROPE_THETA = 10000.0
RMS_EPS = 1e-6
GN_EPS = 1e-5

LANES = 128
SUBLANES_BF16 = 16
VMEM_BYTES_V7X = 64 * 1024 * 1024
VMEM_BUDGET = VMEM_BYTES_V7X - 8 * 1024 * 1024

RET_BLOCK_TARGET = 256


def _pick(total, target, mult):
    best = None
    for d in range(mult, min(total, target) + 1, mult):
        if total % d == 0:
            best = d
    if best is None:
        raise ValueError(f"no tile for {total} (target {target}, multiple of {mult})")
    return best


def _params(semantics, vmem_need):
    limit = min(max(int(vmem_need * 1.2), 16 * 1024 * 1024), VMEM_BUDGET)
    return pltpu.CompilerParams(dimension_semantics=semantics, vmem_limit_bytes=limit)


def _row_parts(ref, parts):
    rows = ref.shape[0]
    sub = rows // parts
    return [slice(p * sub, (p + 1) * sub) for p in range(parts)]


class SideCast:
    def __init__(self, stacked, layer, n_steps, nj):
        _, k, n = stacked.shape
        rows = min(r for r in range(SUBLANES_BF16, k + 1, SUBLANES_BF16) if k % r == 0 and k // r <= n_steps)
        last = k // rows - 1
        self.operand = stacked
        self.out_shape = jax.ShapeDtypeStruct((k, n), BF16)
        self.in_spec = pl.BlockSpec((None, rows, n), lambda i, j, *_: (layer, jnp.minimum(i * nj + j, last), 0))
        self.out_spec = pl.BlockSpec((rows, n), lambda i, j, *_: (jnp.minimum(i * nj + j, last), 0))
        self.vmem = 2 * rows * n * 6


def _run_side_casts(src_refs, dst_refs):
    for src, dst in zip(src_refs, dst_refs):
        dst[...] = src[...].astype(BF16)


def _split_refs(refs, n_in, n_out, n_scratch=0):
    n_cast = (len(refs) - n_in - n_out - n_scratch) // 2
    a, b, c = n_in + n_cast, n_in + n_cast + n_out, n_in + 2 * n_cast + n_out
    return refs[:n_in], refs[a:b], refs[n_in:a], refs[b:c], refs[c:]


def _lane_group_sum(x):
    acc = x[:, :LANES]
    for t in range(1, x.shape[1] // LANES):
        acc = acc + x[:, t * LANES:(t + 1) * LANES]
    return acc


def _store_row_scale(ssq_ref, scale_ref, d_model):
    @pl.when(pl.program_id(1) == 0)
    def _():
        mean_sq = jnp.sum(ssq_ref[...], axis=-1, keepdims=True) / d_model
        scale_ref[...] = jnp.broadcast_to(lax.rsqrt(mean_sq + RMS_EPS), scale_ref.shape)


def _norm_prep_kernel(xp_ref, xs_ref, g_ref, x_ref, xg_ref, ssq_ref, *, head_blocks):
    def emit(src_ref):
        x = src_ref[...]
        x_ref[...] = x
        xg_ref[...] = (x * g_ref[...]).astype(BF16)
        ssq_ref[...] = _lane_group_sum(x * x)

    i = pl.program_id(0)
    pl.when(i < head_blocks)(functools.partial(emit, xp_ref))
    pl.when(i >= head_blocks)(functools.partial(emit, xs_ref))


def norm_prep(x_head, x_tail, g):
    head_rows, d = x_head.shape
    tail_rows = x_tail.shape[0]
    rows = head_rows + tail_rows
    tr = _pick(math.gcd(head_rows, tail_rows), 256, 8)
    hb = head_rows // tr
    return pl.pallas_call(
        functools.partial(_norm_prep_kernel, head_blocks=hb),
        out_shape=(jax.ShapeDtypeStruct((rows, d), F32), jax.ShapeDtypeStruct((rows, d), BF16),
                   jax.ShapeDtypeStruct((rows, LANES), F32)),
        grid=(rows // tr,),
        in_specs=[pl.BlockSpec((tr, d), lambda i: (jnp.minimum(i, hb - 1), 0)),
                  pl.BlockSpec((tr, d), lambda i: (jnp.maximum(i - hb, 0), 0)),
                  pl.BlockSpec((1, d), lambda i: (0, 0))],
        out_specs=(pl.BlockSpec((tr, d), lambda i: (i, 0)),
                   pl.BlockSpec((tr, d), lambda i: (i, 0)),
                   pl.BlockSpec((tr, LANES), lambda i: (i, 0))),
        compiler_params=_params(("arbitrary",), 16 * tr * d * 4),
        name="norm_prep",
    )(x_head, x_tail, g.reshape(1, d))


def _rmsnorm_split_kernel(x_ref, g_ref, head_ref, tail_ref, *, head_blocks):
    x = x_ref[...]
    y = x * lax.rsqrt(jnp.mean(x * x, axis=-1, keepdims=True) + RMS_EPS) * g_ref[...]
    i = pl.program_id(0)

    @pl.when(i < head_blocks)
    def _():
        head_ref[...] = y

    @pl.when(i >= head_blocks)
    def _():
        tail_ref[...] = y


def rmsnorm_split(x, g, head_rows):
    rows, d = x.shape
    assert 0 < head_rows < rows
    tr = _pick(math.gcd(head_rows, rows - head_rows), 256, 8)
    hb = head_rows // tr
    return pl.pallas_call(
        functools.partial(_rmsnorm_split_kernel, head_blocks=hb),
        out_shape=(jax.ShapeDtypeStruct((head_rows, d), F32), jax.ShapeDtypeStruct((rows - head_rows, d), F32)),
        grid=(rows // tr,),
        in_specs=[pl.BlockSpec((tr, d), lambda i: (i, 0)),
                  pl.BlockSpec((1, d), lambda i: (0, 0))],
        out_specs=(pl.BlockSpec((tr, d), lambda i: (jnp.minimum(i, hb - 1), 0)),
                   pl.BlockSpec((tr, d), lambda i: (jnp.maximum(i - hb, 0), 0))),
        compiler_params=_params(("arbitrary",), 10 * tr * d * 4),
        name="rmsnorm_split",
    )(x, g.reshape(1, d))


OP_PLAIN, OP_ROPE, OP_ROPE_SCALED = 0, 1, 2
DST_BF16 = 4


def _inproj_kernel(kind_ref, f32_blk_ref, bf16_blk_ref, *refs, scale, parts, d_model):
    del f32_blk_ref, bf16_blk_ref
    ((xg_ref, w_ref, rope_ref, ssq_ref), (o_ref, ob_ref), cast_src, cast_dst,
     (rinv_ref,)) = _split_refs(refs, 4, 2, 1)
    kind = kind_ref[pl.program_id(1)]
    _store_row_scale(ssq_ref, rinv_ref, d_model)
    groups = [slice(g * LANES, (g + 1) * LANES) for g in range(o_ref.shape[1] // LANES)]

    def plain(dst_ref):
        for rows in _row_parts(dst_ref, parts):
            acc = jnp.dot(xg_ref[rows, :], w_ref[...], preferred_element_type=F32)
            rinv = rinv_ref[rows, :]
            for cols in groups:
                dst_ref[rows, cols] = (acc[:, cols] * rinv).astype(dst_ref.dtype)
        _run_side_casts(cast_src, cast_dst)

    def rotary(dst_ref, s):
        for rows in _row_parts(dst_ref, parts):
            acc = jnp.dot(xg_ref[rows, :], w_ref[...], preferred_element_type=F32)
            rinv = rinv_ref[rows, :]
            c = rope_ref[rows, :LANES]
            sn = rope_ref[rows, LANES:]
            for cols in groups:
                xn = acc[:, cols] * rinv
                r = xn * c + pltpu.roll(xn, LANES // 2, 1) * sn
                dst_ref[rows, cols] = (r * s).astype(dst_ref.dtype)
        _run_side_casts(cast_src, cast_dst)

    pl.when(kind == OP_PLAIN)(functools.partial(plain, o_ref))
    pl.when(kind == OP_PLAIN + DST_BF16)(functools.partial(plain, ob_ref))
    pl.when(kind == OP_ROPE)(functools.partial(rotary, o_ref, F32(1.0)))
    pl.when(kind == OP_ROPE_SCALED)(functools.partial(rotary, o_ref, F32(scale)))
    pl.when(kind == OP_ROPE + DST_BF16)(functools.partial(rotary, ob_ref, F32(1.0)))
    pl.when(kind == OP_ROPE_SCALED + DST_BF16)(functools.partial(rotary, ob_ref, F32(scale)))


def inproj(xg, ssq, w, tile_kinds, rope_t, tn, scale, cast_jobs=()):
    rows, d = xg.shape
    n = w.shape[1]
    tm = _pick(rows, 1536, 8)
    ni, nj = rows // tm, n // tn
    assert len(tile_kinds) == nj
    to_bf16 = [k >= DST_BF16 for k in tile_kinds]
    f32_blk = [max(sum(not t for t in to_bf16[:j + 1]) - 1, 0) for j in range(nj)]
    bf16_blk = [max(sum(to_bf16[:j + 1]) - 1, 0) for j in range(nj)]
    n_bf16 = sum(to_bf16) * tn
    casts = [SideCast(st, layer, ni * nj, nj) for st, layer in cast_jobs]
    need = (2 * (tm * d * 2 + d * tn * 2 + tm * tn * 6 + 3 * tm * LANES * 4) + 5 * tm * LANES * 4
            + sum(c.vmem for c in casts))
    grid_spec = pltpu.PrefetchScalarGridSpec(
        num_scalar_prefetch=3,
        grid=(ni, nj),
        in_specs=[pl.BlockSpec((tm, d), lambda i, j, *_: (i, 0)),
                  pl.BlockSpec((d, tn), lambda i, j, *_: (0, j)),
                  pl.BlockSpec((tm, 2 * LANES), lambda i, j, *_: (i, 0)),
                  pl.BlockSpec((tm, LANES), lambda i, j, *_: (i, 0))] + [c.in_spec for c in casts],
        out_specs=[pl.BlockSpec((tm, tn), lambda i, j, kinds, f32_b, bf16_b: (i, f32_b[j])),
                   pl.BlockSpec((tm, tn), lambda i, j, kinds, f32_b, bf16_b: (i, bf16_b[j]))]
                  + [c.out_spec for c in casts],
        scratch_shapes=[pltpu.VMEM((tm, LANES), F32)],
    )
    out = pl.pallas_call(
        functools.partial(_inproj_kernel, scale=scale, parts=4 if tm % 64 == 0 else 1, d_model=d),
        out_shape=[jax.ShapeDtypeStruct((rows, n - n_bf16), F32), jax.ShapeDtypeStruct((rows, n_bf16), BF16)]
                  + [c.out_shape for c in casts],
        grid_spec=grid_spec,
        compiler_params=_params(("arbitrary", "arbitrary"), need),
        name="inproj_rope",
    )(jnp.asarray(tile_kinds, jnp.int32), jnp.asarray(f32_blk, jnp.int32), jnp.asarray(bf16_blk, jnp.int32),
      xg, w, rope_t, ssq, *[c.operand for c in casts])
    return out[0], out[1], out[2:]


def _sink_column(sink_ref, kvh, group, rows_per_head):
    cols = [jnp.full((rows_per_head, 1), sink_ref[kvh * group + g], F32) for g in range(group)]
    return jnp.concatenate(cols, axis=0)


def _swa_chunk(q_rows, kb, vb, sink_col, group):
    hd = kb.shape[1]
    q = jnp.concatenate([q_rows[:, g * hd:(g + 1) * hd] for g in range(group)], axis=0).astype(BF16)
    s = lax.dot_general(q, kb, (((1,), (1,)), ((), ())), preferred_element_type=F32)
    m = jnp.maximum(jnp.max(s, axis=-1, keepdims=True), sink_col)
    p = jnp.exp(s - m)
    denom = jnp.sum(p, axis=-1, keepdims=True) + jnp.exp(sink_col - m)
    p = (p / denom).astype(BF16)
    return jnp.dot(p, vb, preferred_element_type=F32)


def _swa_prompt_kernel(sink_ref, q_ref, k_ref, v_ref, o_ref, kc_ref, vc_ref, kb_ref, vb_ref, s_ref, p_ref, *,
                       group, window_chunks, pad_chunks, unroll):
    seq, hd = k_ref.shape
    n_chunks = seq // CHUNK
    kvh = pl.program_id(1)
    front = (window_chunks + pad_chunks) * CHUNK
    band = front + CHUNK
    kb_ref[:front, :] = jnp.zeros((front, hd), BF16)
    vb_ref[:front, :] = jnp.zeros((front, hd), BF16)
    kb_ref[front:, :] = k_ref[...].astype(BF16)
    vb_ref[front:, :] = v_ref[...].astype(BF16)
    kc_ref[...] = k_ref[seq - kc_ref.shape[0]:, :]
    vc_ref[...] = v_ref[seq - vc_ref.shape[0]:, :]

    def chunk_rows(c):
        return pl.ds(pl.multiple_of(c * CHUNK, CHUNK), CHUNK)

    def band_rows(c):
        return pl.ds(pl.multiple_of(c * CHUNK, CHUNK), band)

    def scores(c, carry):
        q_rows = q_ref[chunk_rows(c), :]
        q = jnp.concatenate([q_rows[:, g * hd:(g + 1) * hd] for g in range(group)], axis=0).astype(BF16)
        s_ref[c] = lax.dot_general(q, kb_ref[band_rows(c), :], (((1,), (1,)), ((), ())),
                                   preferred_element_type=F32)
        return carry

    band_col = lax.broadcasted_iota(jnp.int32, (CHUNK, band), 1)

    def softmax(c, carry):
        outside = band_col < jnp.maximum(pad_chunks * CHUNK, front - c * CHUNK)
        for g in range(group):
            rows = slice(g * CHUNK, (g + 1) * CHUNK)
            sink = sink_ref[kvh * group + g]
            s = jnp.where(outside, F32(-1e30), s_ref[c, rows, :])
            m = jnp.maximum(jnp.max(s, axis=-1, keepdims=True), sink)
            p = jnp.exp(s - m)
            denom = jnp.sum(p, axis=-1, keepdims=True) + jnp.exp(sink - m)
            p_ref[c, rows, :] = (p / denom).astype(BF16)
        return carry

    def values(c, carry):
        o = jnp.dot(p_ref[c], vb_ref[band_rows(c), :], preferred_element_type=F32)
        for g in range(group):
            o_ref[chunk_rows(c), g * hd:(g + 1) * hd] = o[g * CHUNK:(g + 1) * CHUNK].astype(o_ref.dtype)
        return carry

    lax.fori_loop(0, n_chunks, scores, 0, unroll=2 * unroll)
    lax.fori_loop(0, n_chunks, softmax, 0, unroll=unroll)
    lax.fori_loop(0, n_chunks, values, 0, unroll=2 * unroll)


def _swa_sample_kernel(sink_ref, q_ref, k_ref, v_ref, ck_ref, cv_ref, o_in_ref, o_ref, kn_ref, vn_ref, *, group):
    del o_in_ref
    hd = k_ref.shape[1]
    kn_ref[...] = k_ref[...]
    vn_ref[...] = v_ref[...]
    kb = jnp.concatenate([ck_ref[...], k_ref[...]], axis=0).astype(BF16)
    vb = jnp.concatenate([cv_ref[...], v_ref[...]], axis=0).astype(BF16)
    sink_col = _sink_column(sink_ref, pl.program_id(1), group, q_ref.shape[0])
    o = _swa_chunk(q_ref[...], kb, vb, sink_col, group)
    rows = q_ref.shape[0]
    for g in range(group):
        o_ref[:, g * hd:(g + 1) * hd] = o[g * rows:(g + 1) * rows].astype(o_ref.dtype)


def swa(z32, zb, sinks, cache_k, cache_v, layer, cfg):
    hd, kvh, group = cfg["head_dim"], cfg["kv_heads"], cfg["att_group"]
    b_p, t_p, b_s, t_s = cfg["b_p"], cfg["t_p"], cfg["b_s"], cfg["t_s"]
    rows = z32.shape[0]
    qw = group * hd
    assert cfg["off_q_a"] == 0
    k_blk0 = cfg["off_k_a"] // hd
    v_blk0 = cfg["off_v_a"] // hd
    att_w = cfg["att_width"]
    window = cache_k.shape[2]
    window_chunks = window // CHUNK
    smem = pl.BlockSpec(memory_space=pltpu.SMEM)

    lane_chunks = LANES // CHUNK
    pad_chunks = -(window_chunks + 1) % lane_chunks
    band = (window_chunks + pad_chunks + 1) * CHUNK
    front = band - CHUNK
    n_chunks = t_p // CHUNK
    need_p = (2 * (t_p * qw * 2 + 2 * t_p * hd * 4 + t_p * qw * 2) + 2 * (front + t_p) * hd * 2
              + n_chunks * group * CHUNK * band * 6 + (4 << 20))
    kv_w = kvh * hd
    o, kc_p, vc_p = pl.pallas_call(
        functools.partial(_swa_prompt_kernel, group=group, window_chunks=window_chunks, pad_chunks=pad_chunks,
                          unroll=2),
        out_shape=(jax.ShapeDtypeStruct((rows, att_w), BF16),
                   jax.ShapeDtypeStruct((b_p, window, kv_w), F32),
                   jax.ShapeDtypeStruct((b_p, window, kv_w), F32)),
        grid=(b_p, kvh),
        in_specs=[smem,
                  pl.BlockSpec((t_p, qw), lambda b, k: (b, k)),
                  pl.BlockSpec((t_p, hd), lambda b, k: (b, k_blk0 + k)),
                  pl.BlockSpec((t_p, hd), lambda b, k: (b, v_blk0 + k))],
        out_specs=(pl.BlockSpec((t_p, qw), lambda b, k: (b, k)),
                   pl.BlockSpec((None, window, hd), lambda b, k: (b, 0, k)),
                   pl.BlockSpec((None, window, hd), lambda b, k: (b, 0, k))),
        scratch_shapes=[pltpu.VMEM((front + t_p, hd), BF16), pltpu.VMEM((front + t_p, hd), BF16),
                        pltpu.VMEM((n_chunks, group * CHUNK, band), F32),
                        pltpu.VMEM((n_chunks, group * CHUNK, band), BF16)],
        compiler_params=_params(("parallel", "parallel"), need_p),
        name="swa_prompt",
    )(sinks, zb, z32, z32)

    assert t_s == CHUNK and window + t_s == (window_chunks + 1) * CHUNK
    rb0 = (b_p * t_p) // t_s
    o, kn_s, vn_s = pl.pallas_call(
        functools.partial(_swa_sample_kernel, group=group),
        out_shape=(jax.ShapeDtypeStruct((rows, att_w), BF16),
                   jax.ShapeDtypeStruct((b_s, t_s, kv_w), F32),
                   jax.ShapeDtypeStruct((b_s, t_s, kv_w), F32)),
        grid=(b_s, kvh),
        in_specs=[smem,
                  pl.BlockSpec((t_s, qw), lambda b, k: (rb0 + b, k)),
                  pl.BlockSpec((t_s, hd), lambda b, k: (rb0 + b, k_blk0 + k)),
                  pl.BlockSpec((t_s, hd), lambda b, k: (rb0 + b, v_blk0 + k)),
                  pl.BlockSpec((None, None, window, hd), lambda b, k: (layer, b, 0, k)),
                  pl.BlockSpec((None, None, window, hd), lambda b, k: (layer, b, 0, k)),
                  pl.BlockSpec(memory_space=pl.ANY)],
        out_specs=(pl.BlockSpec((t_s, qw), lambda b, k: (rb0 + b, k)),
                   pl.BlockSpec((None, t_s, hd), lambda b, k: (b, 0, k)),
                   pl.BlockSpec((None, t_s, hd), lambda b, k: (b, 0, k))),
        input_output_aliases={6: 0},
        compiler_params=_params(("parallel", "parallel"), 8 << 20),
        name="swa_sample",
    )(sinks, zb, z32, z32, cache_k, cache_v, o)
    return o, (kc_p, vc_p, kn_s, vn_s)


def _retention_state_update(k32, v_bf, zeta):
    kz_t = jnp.transpose(k32 * zeta).astype(BF16)
    return jnp.dot(kz_t, v_bf, preferred_element_type=F32)


def _retention_readout(q_bf, k_bf, v_bf, g32, state_bf, decay, xi, gain):
    s = lax.dot_general(q_bf, k_bf, (((1,), (1,)), ((), ())), preferred_element_type=F32) * decay
    o = (jnp.dot(s.astype(BF16), v_bf, preferred_element_type=F32)
         + jnp.dot(q_bf, state_bf, preferred_element_type=F32) * xi)
    mu = jnp.mean(o, axis=-1, keepdims=True)
    var = jnp.mean(jnp.square(o - mu), axis=-1, keepdims=True)
    on = (o - mu) * lax.rsqrt(var + GN_EPS) * gain
    return (g32 * jax.nn.sigmoid(g32)) * on


def _retention_prompt_kernel(cd_ref, q_ref, k_ref, v_ref, g_ref, decay_ref, xi_ref, zeta_ref, gain_ref,
                             o_ref, st_ref, sb_ref, *, block, unroll):
    h = pl.program_id(1)
    n_blocks = q_ref.shape[0] // block
    block_decay = cd_ref[h]
    zeta = zeta_ref[...]

    state = jnp.zeros(st_ref.shape, F32)
    for c in range(n_blocks):
        rows = slice(c * block, (c + 1) * block)
        sb_ref[c] = state.astype(BF16)
        state = block_decay * state + _retention_state_update(k_ref[rows, :], v_ref[rows, :].astype(BF16), zeta)
    st_ref[...] = state

    gain = gain_ref[pl.ds(h, 1), :]

    def body(t, carry):
        for u in range(unroll):
            c = t * unroll + u
            rows = pl.ds(pl.multiple_of(c * block, block), block)
            out = _retention_readout(q_ref[rows, :].astype(BF16), k_ref[rows, :].astype(BF16),
                                     v_ref[rows, :].astype(BF16), g_ref[rows, :].astype(F32), sb_ref[c],
                                     decay_ref[...], xi_ref[...], gain)
            o_ref[rows, :] = out.astype(o_ref.dtype)
        return carry

    lax.fori_loop(0, n_blocks // unroll, body, 0)


def _retention_sample_kernel(cd_ref, q_ref, k_ref, v_ref, g_ref, decay_ref, xi_ref, zeta_ref, gain_ref,
                             st_in_ref, o_in_ref, o_ref, st_ref):
    del o_in_ref
    n_heads, dk, dv = st_in_ref.shape
    head0 = pl.program_id(1) * n_heads
    for i in range(n_heads):
        qk_cols, v_cols = slice(i * dk, (i + 1) * dk), slice(i * dv, (i + 1) * dv)
        state = st_in_ref[i]
        k32 = k_ref[:, qk_cols]
        v_bf = v_ref[:, v_cols].astype(BF16)
        out = _retention_readout(q_ref[:, qk_cols].astype(BF16), k32.astype(BF16), v_bf, g_ref[:, v_cols].astype(F32),
                                 state.astype(BF16), decay_ref[i], xi_ref[i], gain_ref[pl.ds(head0 + i, 1), :])
        o_ref[:, v_cols] = out.astype(o_ref.dtype)
        st_ref[i] = cd_ref[head0 + i] * state + _retention_state_update(k32, v_bf, zeta_ref[i])


def retention(z32, zb, state_in, gain, layer, consts_p, consts_s, cfg):
    dk, dv, heads = cfg["ret_dk"], cfg["ret_dv"], cfg["ret_heads"]
    b_p, t_p, b_s, t_s = cfg["b_p"], cfg["t_p"], cfg["b_s"], cfg["t_s"]
    rows = z32.shape[0]
    smem = pl.BlockSpec(memory_space=pltpu.SMEM)
    col_offsets = (("off_q_r", dk), ("off_k_r", dk), ("off_v_rb", dv), ("off_g_rb", dv))

    def specs(t, length, hg, row_block0):
        lead, hg = hg, hg or 1
        col0 = [cfg[o] // (hg * w) for o, w in col_offsets]
        return ([pl.BlockSpec((t, hg * w), lambda b, h, c0=c0: (row_block0 + b, c0 + h))
                 for c0, (_, w) in zip(col0, col_offsets)] +
                [pl.BlockSpec((lead, length, length), lambda b, h: (h, 0, 0)),
                 pl.BlockSpec((lead, length, 1), lambda b, h: (h, 0, 0)),
                 pl.BlockSpec((lead, length, 1), lambda b, h: (h, 0, 0)),
                 pl.BlockSpec((heads, dv), lambda b, h: (0, 0))])

    decay, xi, zeta, block_decay = consts_p
    block = decay.shape[1]
    n_blocks = t_p // block
    unroll = max(u for u in (1, 2, 4, 8) if n_blocks % u == 0)
    need_p = (2 * (2 * t_p * dk * 4 + 2 * t_p * dv * 4 + t_p * dv * 2 + block * block * 4)
              + n_blocks * dk * dv * 2 + (8 << 20))
    rr, st_p = pl.pallas_call(
        functools.partial(_retention_prompt_kernel, block=block, unroll=unroll),
        out_shape=(jax.ShapeDtypeStruct((rows, heads * dv), BF16),
                   jax.ShapeDtypeStruct((b_p, heads, dk, dv), F32)),
        grid=(b_p, heads),
        in_specs=[smem] + specs(t_p, block, None, 0),
        out_specs=(pl.BlockSpec((t_p, dv), lambda b, h: (b, h)),
                   pl.BlockSpec((None, None, dk, dv), lambda b, h: (b, h, 0, 0))),
        scratch_shapes=[pltpu.VMEM((n_blocks, dk, dv), BF16)],
        compiler_params=_params(("parallel", "parallel"), need_p),
        name="retention_prompt",
    )(block_decay, zb, z32, zb, zb, decay, xi, zeta, gain)

    decay, xi, zeta, block_decay = consts_s
    assert decay.shape[1] == t_s
    rb0 = (b_p * t_p) // t_s
    hg = max(n for n in range(1, heads + 1)
             if heads % n == 0 and all(cfg[o] % (n * w) == 0 for o, w in col_offsets))
    rr, st_s = pl.pallas_call(
        _retention_sample_kernel,
        out_shape=(jax.ShapeDtypeStruct((rows, heads * dv), BF16),
                   jax.ShapeDtypeStruct((b_s, heads, dk, dv), F32)),
        grid=(b_s, heads // hg),
        in_specs=[smem] + specs(t_s, t_s, hg, rb0) +
                 [pl.BlockSpec((None, None, hg, dk, dv), lambda b, h: (layer, b, h, 0, 0)),
                  pl.BlockSpec(memory_space=pl.ANY)],
        out_specs=(pl.BlockSpec((t_s, hg * dv), lambda b, h: (rb0 + b, h)),
                   pl.BlockSpec((None, hg, dk, dv), lambda b, h: (b, h, 0, 0))),
        input_output_aliases={10: 0},
        compiler_params=_params(("parallel", "parallel"), 16 << 20),
        name="retention_sample",
    )(block_decay, zb, z32, zb, zb, decay, xi, zeta, gain, state_in, rr)
    return rr, st_p, st_s


def _merge_kernel(*refs, parts):
    (oa_ref, rr_ref, wa_ref, wr_ref, ga_ref, gr_ref), (o_ref,), cast_src, cast_dst, _ = _split_refs(refs, 6, 1)
    for rows in _row_parts(o_ref, parts):
        branch_a = jnp.dot(oa_ref[rows, :], wa_ref[...], preferred_element_type=F32)
        branch_r = jnp.dot(rr_ref[rows, :], wr_ref[...], preferred_element_type=F32)
        merged = (jax.nn.sigmoid(ga_ref[rows, :].astype(F32)) * branch_a
                  + jax.nn.sigmoid(gr_ref[rows, :].astype(F32)) * branch_r)
        o_ref[rows, :] = merged.astype(o_ref.dtype)
    _run_side_casts(cast_src, cast_dst)


def merge_branches(o_a, rr, w_a, w_r, zb, cfg, cast_jobs=()):
    rows, ka = o_a.shape
    kr = rr.shape[1]
    d = w_a.shape[1]
    tm = _pick(rows, 768, 8)
    tn = _pick(d, 1024, LANES)
    ni, nj = rows // tm, d // tn
    casts = [SideCast(st, layer, ni * nj, nj) for st, layer in cast_jobs]
    ga0, gr0 = cfg["off_gate_ab"] // tn, cfg["off_gate_rb"] // tn
    assert cfg["off_gate_ab"] % tn == 0 and cfg["off_gate_rb"] % tn == 0
    need = (2 * (tm * (ka + kr) * 2 + (ka + kr) * tn * 2 + 3 * tm * tn * 2) + 4 * (tm // 4) * tn * 4
            + sum(c.vmem for c in casts))
    out = pl.pallas_call(
        functools.partial(_merge_kernel, parts=4 if tm % 64 == 0 else 1),
        out_shape=[jax.ShapeDtypeStruct((rows, d), BF16)] + [c.out_shape for c in casts],
        grid=(ni, nj),
        in_specs=[pl.BlockSpec((tm, ka), lambda i, j: (i, 0)),
                  pl.BlockSpec((tm, kr), lambda i, j: (i, 0)),
                  pl.BlockSpec((ka, tn), lambda i, j: (0, j)),
                  pl.BlockSpec((kr, tn), lambda i, j: (0, j)),
                  pl.BlockSpec((tm, tn), lambda i, j: (i, ga0 + j)),
                  pl.BlockSpec((tm, tn), lambda i, j: (i, gr0 + j))] + [c.in_spec for c in casts],
        out_specs=[pl.BlockSpec((tm, tn), lambda i, j: (i, j))] + [c.out_spec for c in casts],
        compiler_params=_params(("arbitrary", "arbitrary"), need),
        name="merge_branches",
    )(o_a, rr, w_a, w_r, zb, zb, *[c.operand for c in casts])
    return out[0], out[1:]


def _matmul_residual_kernel(*refs, parts, fold_norm):
    ins, outs, cast_src, cast_dst, _ = _split_refs(refs, 4 if fold_norm else 3, 3 if fold_norm else 1)
    a_ref, w_ref, x_ref = ins[:3]
    o_ref = outs[0]
    if fold_norm:
        g_ref, (xg_ref, ssq_ref) = ins[3], outs[1:]

        @pl.when(pl.program_id(1) == 0)
        def _():
            ssq_ref[...] = jnp.zeros(ssq_ref.shape, F32)

    for rows in _row_parts(o_ref, parts):
        y = x_ref[rows, :] + jnp.dot(a_ref[rows, :], w_ref[...], preferred_element_type=F32)
        o_ref[rows, :] = y
        if fold_norm:
            xg_ref[rows, :] = (y * g_ref[...]).astype(BF16)
            ssq_ref[rows, :] += _lane_group_sum(y * y)
    _run_side_casts(cast_src, cast_dst)


def matmul_residual(a, w, x, tm_target, tn_target, next_gain=None, cast_jobs=()):
    rows, k = a.shape
    n = w.shape[1]
    tm = _pick(rows, tm_target, 8)
    tn = _pick(n, tn_target, LANES)
    ni, nj = rows // tm, n // tn
    fold_norm = next_gain is not None
    casts = [SideCast(st, layer, ni * nj, nj) for st, layer in cast_jobs]
    need = (2 * (tm * k * 2 + k * tn * 2 + 2 * tm * tn * 4 + tm * tn * 2 + tm * LANES * 4) + 3 * tm * tn * 4
            + sum(c.vmem for c in casts))
    tile = pl.BlockSpec((tm, tn), lambda i, j: (i, j))
    in_specs = [pl.BlockSpec((tm, k), lambda i, j: (i, 0)), pl.BlockSpec((k, tn), lambda i, j: (0, j)), tile]
    out_specs, out_shape, operands = [tile], [jax.ShapeDtypeStruct((rows, n), F32)], [a, w, x]
    if fold_norm:
        in_specs.append(pl.BlockSpec((1, tn), lambda i, j: (0, j)))
        operands.append(next_gain.reshape(1, n))
        out_specs += [tile, pl.BlockSpec((tm, LANES), lambda i, j: (i, 0))]
        out_shape += [jax.ShapeDtypeStruct((rows, n), BF16), jax.ShapeDtypeStruct((rows, LANES), F32)]
    out = pl.pallas_call(
        functools.partial(_matmul_residual_kernel, parts=4 if tm % 64 == 0 else 1, fold_norm=fold_norm),
        out_shape=out_shape + [c.out_shape for c in casts],
        grid=(ni, nj),
        in_specs=in_specs + [c.in_spec for c in casts],
        out_specs=out_specs + [c.out_spec for c in casts],
        compiler_params=_params(("arbitrary", "arbitrary"), need),
        name="matmul_residual",
    )(*operands, *[c.operand for c in casts])
    n_own = len(out_shape)
    return out[:n_own], out[n_own:]


def _gate_up_kernel(*refs, parts, d_model):
    (xg_ref, wg_ref, wu_ref, ssq_ref), (o_ref,), cast_src, cast_dst, (rinv_ref,) = _split_refs(refs, 4, 1, 1)
    _store_row_scale(ssq_ref, rinv_ref, d_model)
    for rows in _row_parts(o_ref, parts):
        xg = xg_ref[rows, :]
        rinv = rinv_ref[rows, :]
        gate_acc = jnp.dot(xg, wg_ref[...], preferred_element_type=F32)
        up_acc = jnp.dot(xg, wu_ref[...], preferred_element_type=F32)
        for g in range(o_ref.shape[1] // LANES):
            cols = slice(g * LANES, (g + 1) * LANES)
            gate = gate_acc[:, cols] * rinv
            up = up_acc[:, cols] * rinv
            o_ref[rows, cols] = ((gate * jax.nn.sigmoid(gate)) * up).astype(o_ref.dtype)
    _run_side_casts(cast_src, cast_dst)


def gate_up(xg, ssq, w_gate_up, cast_jobs=()):
    rows, d = xg.shape
    f = w_gate_up.shape[1] // 2
    tm = _pick(rows, 1536, 8)
    tn = _pick(f, 256, LANES)
    ni, nj = rows // tm, f // tn
    casts = [SideCast(st, layer, ni * nj, nj) for st, layer in cast_jobs]
    need = (2 * (tm * d * 2 + 2 * d * tn * 2 + tm * tn * 2 + tm * LANES * 4) + 4 * tm * tn * 4 + tm * LANES * 4
            + sum(c.vmem for c in casts))
    out = pl.pallas_call(
        functools.partial(_gate_up_kernel, parts=2 if tm % 16 == 0 else 1, d_model=d),
        out_shape=[jax.ShapeDtypeStruct((rows, f), BF16)] + [c.out_shape for c in casts],
        grid=(ni, nj),
        in_specs=[pl.BlockSpec((tm, d), lambda i, j: (i, 0)),
                  pl.BlockSpec((d, tn), lambda i, j: (0, j)),
                  pl.BlockSpec((d, tn), lambda i, j: (0, nj + j)),
                  pl.BlockSpec((tm, LANES), lambda i, j: (i, 0))] + [c.in_spec for c in casts],
        out_specs=[pl.BlockSpec((tm, tn), lambda i, j: (i, j))] + [c.out_spec for c in casts],
        scratch_shapes=[pltpu.VMEM((tm, LANES), F32)],
        compiler_params=_params(("arbitrary", "arbitrary"), need),
        name="swiglu_gate_up",
    )(xg, w_gate_up, w_gate_up, ssq, *[c.operand for c in casts])
    return out[0], out[1:]


def _rope_tables(positions, hd):
    half = hd // 2
    inv = ROPE_THETA ** (-jnp.arange(half, dtype=F32) / half)
    ang = positions.astype(F32)[:, None] * inv[None, :]
    cos, sin = jnp.cos(ang), jnp.sin(ang)
    return jnp.concatenate([cos, cos, -sin, sin], axis=-1)


def _retention_tables(heads, length):
    lg = jnp.log1p(-jnp.exp2(-5.0 - jnp.arange(heads, dtype=F32)))
    j = jnp.arange(length, dtype=F32)
    rel = j[:, None] - j[None, :]
    decay = jnp.where(rel >= 0, jnp.exp(jnp.maximum(rel, 0.0)[None] * lg[:, None, None]), 0.0)
    xi = jnp.exp((j + 1.0)[None, :] * lg[:, None])[:, :, None]
    zeta = jnp.exp((length - 1.0 - j)[None, :] * lg[:, None])[:, :, None]
    block_decay = jnp.exp(length * lg)
    return decay, xi, zeta, block_decay


def kernel(x_prompt, x_sample, cache_swa_k, cache_swa_v, state_retention, w_in, w_proj_a, w_proj_r, w_out,
           attn_sinks, ret_norm_gain, norm_mix, norm_ffn, w_gate_up, w_down, norm_final):
    b_p, t_p, d = x_prompt.shape
    b_s, t_s, _ = x_sample.shape
    depth = w_in.shape[0]
    _, _, window, kvh, hd = cache_swa_k.shape
    att_heads = attn_sinks.shape[1]
    _, _, ret_heads, ret_dk, ret_dv = state_retention.shape
    att_w, kv_w = att_heads * hd, kvh * hd
    rqk_w, rv_w = ret_heads * ret_dk, ret_heads * ret_dv
    splits = (att_w, kv_w, kv_w, rqk_w, rqk_w, rv_w, rv_w, d, d)
    offs = [0]
    for s in splits:
        offs.append(offs[-1] + s)
    assert offs[-1] == w_in.shape[2] and hd == LANES and ret_dk == LANES
    split_kind = (OP_ROPE_SCALED + DST_BF16, OP_ROPE, OP_PLAIN, OP_ROPE + DST_BF16, OP_ROPE_SCALED,
                  OP_PLAIN + DST_BF16, OP_PLAIN + DST_BF16, OP_PLAIN + DST_BF16, OP_PLAIN + DST_BF16)
    assert hd == ret_dk
    tn_in = _pick(kv_w, 512, LANES)
    assert all(o % tn_in == 0 for o in offs)
    tile_kinds = [kd for kd, s in zip(split_kind, splits) for _ in range(s // tn_in)]
    off_in, fill = [], {False: 0, True: 0}
    for kd, s in zip(split_kind, splits):
        off_in.append(fill[kd >= DST_BF16])
        fill[kd >= DST_BF16] += s
    cfg = dict(b_p=b_p, t_p=t_p, b_s=b_s, t_s=t_s, head_dim=hd, kv_heads=kvh, att_group=att_heads // kvh,
               att_width=att_w, ret_heads=ret_heads, ret_dk=ret_dk, ret_dv=ret_dv,
               off_q_a=off_in[0], off_k_a=off_in[1], off_v_a=off_in[2], off_q_r=off_in[3], off_k_r=off_in[4],
               off_v_rb=off_in[5], off_g_rb=off_in[6], off_gate_ab=off_in[7], off_gate_rb=off_in[8])

    pos = jnp.concatenate([jnp.tile(jnp.arange(t_p, dtype=jnp.int32), b_p),
                           jnp.tile(PAST_LEN + jnp.arange(t_s, dtype=jnp.int32), b_s)])
    rope_t = _rope_tables(pos, hd)
    ret_consts_p = _retention_tables(ret_heads, _pick(t_p, RET_BLOCK_TARGET, CHUNK))
    ret_consts_s = _retention_tables(ret_heads, t_s)

    cache_k = cache_swa_k.reshape(depth, b_s, window, kv_w)
    cache_v = cache_swa_v.reshape(depth, b_s, window, kv_w)
    w_in_b, w_a_b, w_r_b, w_out_b, w_down_b = (w[0].astype(BF16) for w in (w_in, w_proj_a, w_proj_r, w_out, w_down))

    rows_p = b_p * t_p
    x, xg, ssq = norm_prep(x_prompt.reshape(rows_p, d), x_sample.reshape(b_s * t_s, d), norm_mix[0])

    kp_rows, vp_rows, sp_states, ks_rows, vs_rows, ss_states = [], [], [], [], [], []
    for layer in range(depth):
        nxt = layer + 1

        def next_layer(*stacked):
            return [(w, nxt) for w in stacked] if nxt < depth else []

        z32, zb, (w_gu_b,) = inproj(xg, ssq, w_in_b, tile_kinds, rope_t, tn_in, hd ** -0.5, [(w_gate_up, layer)])
        o_a, (kc_p, vc_p, kn_s, vn_s) = swa(z32, zb, attn_sinks[layer], cache_k, cache_v, layer, cfg)
        rr, st_p, st_s = retention(z32, zb, state_retention, ret_norm_gain[layer], layer,
                                   ret_consts_p, ret_consts_s, cfg)
        merged, _ = merge_branches(o_a, rr, w_a_b, w_r_b, zb, cfg)
        (x, xg, ssq), next_proj = matmul_residual(merged, w_out_b, x, 768, 1024, norm_ffn[layer],
                                                  next_layer(w_out, w_proj_a, w_proj_r))
        act, next_in = gate_up(xg, ssq, w_gu_b, next_layer(w_in))
        if nxt < depth:
            (x, xg, ssq), next_down = matmul_residual(act, w_down_b, x, 768, 256, norm_mix[nxt], next_layer(w_down))
            (w_out_b, w_a_b, w_r_b), (w_in_b,), (w_down_b,) = next_proj, next_in, next_down
        else:
            (x,), _ = matmul_residual(act, w_down_b, x, 768, 256)

        kp_rows.append(kc_p.reshape(b_p, window, kvh, hd))
        vp_rows.append(vc_p.reshape(b_p, window, kvh, hd))
        ks_rows.append(kn_s.reshape(b_s, t_s, kvh, hd))
        vs_rows.append(vn_s.reshape(b_s, t_s, kvh, hd))
        sp_states.append(st_p)
        ss_states.append(st_s)

    y_p, y_s = rmsnorm_split(x, norm_final, rows_p)
    return (y_p.reshape(b_p, t_p, d), y_s.reshape(b_s, t_s, d),
            jnp.stack(kp_rows), jnp.stack(vp_rows), jnp.stack(sp_states),
            jnp.stack(ks_rows), jnp.stack(vs_rows), jnp.stack(ss_states))
```

```python
import functools
import math

import jax
import jax.numpy as jnp
from jax import lax
from jax.experimental import pallas as pl
from jax.experimental.pallas import tpu as pltpu

F32 = jnp.float32
BF16 = jnp.bfloat16

CHUNK = 64
PAST_LEN = 2048
ROPE_THETA = 10000.0
RMS_EPS = 1e-6
GN_EPS = 1e-5

LANES = 128
SUBLANES_BF16 = 16
VMEM_BYTES_V7X = 64 * 1024 * 1024
VMEM_BUDGET = VMEM_BYTES_V7X - 8 * 1024 * 1024

RET_BLOCK_TARGET = 256


def _pick(total, target, mult):
    best = None
    for d in range(mult, min(total, target) + 1, mult):
        if total % d == 0:
            best = d
    if best is None:
        raise ValueError(f"no tile for {total} (target {target}, multiple of {mult})")
    return best


def _params(semantics, vmem_need):
    limit = min(max(int(vmem_need * 1.2), 16 * 1024 * 1024), VMEM_BUDGET)
    return pltpu.CompilerParams(dimension_semantics=semantics, vmem_limit_bytes=limit)


def _row_parts(ref, parts):
    rows = ref.shape[0]
    sub = rows // parts
    return [slice(p * sub, (p + 1) * sub) for p in range(parts)]


class SideCast:
    def __init__(self, stacked, layer, n_steps, nj):
        _, k, n = stacked.shape
        rows = min(r for r in range(SUBLANES_BF16, k + 1, SUBLANES_BF16) if k % r == 0 and k // r <= n_steps)
        last = k // rows - 1
        self.operand = stacked
        self.out_shape = jax.ShapeDtypeStruct((k, n), BF16)
        self.in_spec = pl.BlockSpec((None, rows, n), lambda i, j, *_: (layer, jnp.minimum(i * nj + j, last), 0))
        self.out_spec = pl.BlockSpec((rows, n), lambda i, j, *_: (jnp.minimum(i * nj + j, last), 0))
        self.vmem = 2 * rows * n * 6


def _run_side_casts(src_refs, dst_refs):
    for src, dst in zip(src_refs, dst_refs):
        dst[...] = src[...].astype(BF16)


def _split_refs(refs, n_in, n_out, n_scratch=0):
    n_cast = (len(refs) - n_in - n_out - n_scratch) // 2
    a, b, c = n_in + n_cast, n_in + n_cast + n_out, n_in + 2 * n_cast + n_out
    return refs[:n_in], refs[a:b], refs[n_in:a], refs[b:c], refs[c:]


def _lane_group_sum(x):
    acc = x[:, :LANES]
    for t in range(1, x.shape[1] // LANES):
        acc = acc + x[:, t * LANES:(t + 1) * LANES]
    return acc


def _store_row_scale(ssq_ref, scale_ref, d_model):
    @pl.when(pl.program_id(1) == 0)
    def _():
        mean_sq = jnp.sum(ssq_ref[...], axis=-1, keepdims=True) / d_model
        scale_ref[...] = jnp.broadcast_to(lax.rsqrt(mean_sq + RMS_EPS), scale_ref.shape)


def _norm_prep_kernel(xp_ref, xs_ref, g_ref, x_ref, xg_ref, ssq_ref, *, head_blocks):
    def emit(src_ref):
        x = src_ref[...]
        x_ref[...] = x
        xg_ref[...] = (x * g_ref[...]).astype(BF16)
        ssq_ref[...] = _lane_group_sum(x * x)

    i = pl.program_id(0)
    pl.when(i < head_blocks)(functools.partial(emit, xp_ref))
    pl.when(i >= head_blocks)(functools.partial(emit, xs_ref))


def norm_prep(x_head, x_tail, g):
    head_rows, d = x_head.shape
    tail_rows = x_tail.shape[0]
    rows = head_rows + tail_rows
    tr = _pick(math.gcd(head_rows, tail_rows), 256, 8)
    hb = head_rows // tr
    return pl.pallas_call(
        functools.partial(_norm_prep_kernel, head_blocks=hb),
        out_shape=(jax.ShapeDtypeStruct((rows, d), F32), jax.ShapeDtypeStruct((rows, d), BF16),
                   jax.ShapeDtypeStruct((rows, LANES), F32)),
        grid=(rows // tr,),
        in_specs=[pl.BlockSpec((tr, d), lambda i: (jnp.minimum(i, hb - 1), 0)),
                  pl.BlockSpec((tr, d), lambda i: (jnp.maximum(i - hb, 0), 0)),
                  pl.BlockSpec((1, d), lambda i: (0, 0))],
        out_specs=(pl.BlockSpec((tr, d), lambda i: (i, 0)),
                   pl.BlockSpec((tr, d), lambda i: (i, 0)),
                   pl.BlockSpec((tr, LANES), lambda i: (i, 0))),
        compiler_params=_params(("arbitrary",), 16 * tr * d * 4),
        name="norm_prep",
    )(x_head, x_tail, g.reshape(1, d))


def _rmsnorm_split_kernel(x_ref, g_ref, head_ref, tail_ref, *, head_blocks):
    x = x_ref[...]
    y = x * lax.rsqrt(jnp.mean(x * x, axis=-1, keepdims=True) + RMS_EPS) * g_ref[...]
    i = pl.program_id(0)

    @pl.when(i < head_blocks)
    def _():
        head_ref[...] = y

    @pl.when(i >= head_blocks)
    def _():
        tail_ref[...] = y


def rmsnorm_split(x, g, head_rows):
    rows, d = x.shape
    assert 0 < head_rows < rows
    tr = _pick(math.gcd(head_rows, rows - head_rows), 256, 8)
    hb = head_rows // tr
    return pl.pallas_call(
        functools.partial(_rmsnorm_split_kernel, head_blocks=hb),
        out_shape=(jax.ShapeDtypeStruct((head_rows, d), F32), jax.ShapeDtypeStruct((rows - head_rows, d), F32)),
        grid=(rows // tr,),
        in_specs=[pl.BlockSpec((tr, d), lambda i: (i, 0)),
                  pl.BlockSpec((1, d), lambda i: (0, 0))],
        out_specs=(pl.BlockSpec((tr, d), lambda i: (jnp.minimum(i, hb - 1), 0)),
                   pl.BlockSpec((tr, d), lambda i: (jnp.maximum(i - hb, 0), 0))),
        compiler_params=_params(("arbitrary",), 10 * tr * d * 4),
        name="rmsnorm_split",
    )(x, g.reshape(1, d))


OP_PLAIN, OP_ROPE, OP_ROPE_SCALED = 0, 1, 2
DST_BF16 = 4


def _inproj_kernel(kind_ref, f32_blk_ref, bf16_blk_ref, *refs, scale, parts, d_model):
    del f32_blk_ref, bf16_blk_ref
    ((xg_ref, w_ref, rope_ref, ssq_ref), (o_ref, ob_ref), cast_src, cast_dst,
     (rinv_ref,)) = _split_refs(refs, 4, 2, 1)
    kind = kind_ref[pl.program_id(1)]
    _store_row_scale(ssq_ref, rinv_ref, d_model)
    groups = [slice(g * LANES, (g + 1) * LANES) for g in range(o_ref.shape[1] // LANES)]

    def plain(dst_ref):
        for rows in _row_parts(dst_ref, parts):
            acc = jnp.dot(xg_ref[rows, :], w_ref[...], preferred_element_type=F32)
            rinv = rinv_ref[rows, :]
            for cols in groups:
                dst_ref[rows, cols] = (acc[:, cols] * rinv).astype(dst_ref.dtype)
        _run_side_casts(cast_src, cast_dst)

    def rotary(dst_ref):
        s = jnp.where(kind % DST_BF16 == OP_ROPE_SCALED, F32(scale), F32(1.0))
        for rows in _row_parts(dst_ref, parts):
            acc = jnp.dot(xg_ref[rows, :], w_ref[...], preferred_element_type=F32)
            rinv = rinv_ref[rows, :]
            c = rope_ref[rows, :LANES]
            sn = rope_ref[rows, LANES:]
            for cols in groups:
                xn = acc[:, cols] * rinv
                r = xn * c + pltpu.roll(xn, LANES // 2, 1) * sn
                dst_ref[rows, cols] = (r * s).astype(dst_ref.dtype)
        _run_side_casts(cast_src, cast_dst)

    pl.when(kind == OP_PLAIN)(functools.partial(plain, o_ref))
    pl.when(kind == OP_PLAIN + DST_BF16)(functools.partial(plain, ob_ref))
    pl.when((kind == OP_ROPE) | (kind == OP_ROPE_SCALED))(functools.partial(rotary, o_ref))
    pl.when((kind == OP_ROPE + DST_BF16) | (kind == OP_ROPE_SCALED + DST_BF16))(functools.partial(rotary, ob_ref))


def inproj(xg, ssq, w, tile_kinds, rope_t, tn, scale, cast_jobs=()):
    rows, d = xg.shape
    n = w.shape[1]
    tm = _pick(rows, 1536, 8)
    ni, nj = rows // tm, n // tn
    assert len(tile_kinds) == nj
    to_bf16 = [k >= DST_BF16 for k in tile_kinds]
    f32_blk = [max(sum(not t for t in to_bf16[:j + 1]) - 1, 0) for j in range(nj)]
    bf16_blk = [max(sum(to_bf16[:j + 1]) - 1, 0) for j in range(nj)]
    n_bf16 = sum(to_bf16) * tn
    casts = [SideCast(st, layer, ni * nj, nj) for st, layer in cast_jobs]
    need = (2 * (tm * d * 2 + d * tn * 2 + tm * tn * 6 + 3 * tm * LANES * 4) + 5 * tm * LANES * 4
            + sum(c.vmem for c in casts))
    grid_spec = pltpu.PrefetchScalarGridSpec(
        num_scalar_prefetch=3,
        grid=(ni, nj),
        in_specs=[pl.BlockSpec((tm, d), lambda i, j, *_: (i, 0)),
                  pl.BlockSpec((d, tn), lambda i, j, *_: (0, j)),
                  pl.BlockSpec((tm, 2 * LANES), lambda i, j, *_: (i, 0)),
                  pl.BlockSpec((tm, LANES), lambda i, j, *_: (i, 0))] + [c.in_spec for c in casts],
        out_specs=[pl.BlockSpec((tm, tn), lambda i, j, kinds, f32_b, bf16_b: (i, f32_b[j])),
                   pl.BlockSpec((tm, tn), lambda i, j, kinds, f32_b, bf16_b: (i, bf16_b[j]))]
                  + [c.out_spec for c in casts],
        scratch_shapes=[pltpu.VMEM((tm, LANES), F32)],
    )
    out = pl.pallas_call(
        functools.partial(_inproj_kernel, scale=scale, parts=4 if tm % 64 == 0 else 1, d_model=d),
        out_shape=[jax.ShapeDtypeStruct((rows, n - n_bf16), F32), jax.ShapeDtypeStruct((rows, n_bf16), BF16)]
                  + [c.out_shape for c in casts],
        grid_spec=grid_spec,
        compiler_params=_params(("arbitrary", "arbitrary"), need),
        name="inproj_rope",
    )(jnp.asarray(tile_kinds, jnp.int32), jnp.asarray(f32_blk, jnp.int32), jnp.asarray(bf16_blk, jnp.int32),
      xg, w, rope_t, ssq, *[c.operand for c in casts])
    return out[0], out[1], out[2:]


def _sink_column(sink_ref, kvh, group, rows_per_head):
    cols = [jnp.full((rows_per_head, 1), sink_ref[kvh * group + g], F32) for g in range(group)]
    return jnp.concatenate(cols, axis=0)


def _swa_chunk(q_rows, kb, vb, sink_col, group):
    hd = kb.shape[1]
    q = jnp.concatenate([q_rows[:, g * hd:(g + 1) * hd] for g in range(group)], axis=0).astype(BF16)
    s = lax.dot_general(q, kb, (((1,), (1,)), ((), ())), preferred_element_type=F32)
    m = jnp.maximum(jnp.max(s, axis=-1, keepdims=True), sink_col)
    p = jnp.exp(s - m)
    denom = jnp.sum(p, axis=-1, keepdims=True) + jnp.exp(sink_col - m)
    p = (p / denom).astype(BF16)
    return jnp.dot(p, vb, preferred_element_type=F32)


def _swa_prompt_kernel(sink_ref, q_ref, k_ref, v_ref, o_ref, kc_ref, vc_ref, kb_ref, vb_ref, s_ref, p_ref, *,
                       group, window_chunks, pad_chunks, unroll):
    seq, hd = k_ref.shape
    n_chunks = seq // CHUNK
    kvh = pl.program_id(1)
    front = (window_chunks + pad_chunks) * CHUNK
    band = front + CHUNK
    kb_ref[:front, :] = jnp.zeros((front, hd), BF16)
    vb_ref[:front, :] = jnp.zeros((front, hd), BF16)
    kb_ref[front:, :] = k_ref[...].astype(BF16)
    vb_ref[front:, :] = v_ref[...].astype(BF16)
    kc_ref[...] = k_ref[seq - kc_ref.shape[0]:, :]
    vc_ref[...] = v_ref[seq - vc_ref.shape[0]:, :]

    def chunk_rows(c):
        return pl.ds(pl.multiple_of(c * CHUNK, CHUNK), CHUNK)

    def band_rows(c):
        return pl.ds(pl.multiple_of(c * CHUNK, CHUNK), band)

    def scores(c, carry):
        q_rows = q_ref[chunk_rows(c), :]
        q = jnp.concatenate([q_rows[:, g * hd:(g + 1) * hd] for g in range(group)], axis=0).astype(BF16)
        s_ref[c] = lax.dot_general(q, kb_ref[band_rows(c), :], (((1,), (1,)), ((), ())),
                                   preferred_element_type=F32)
        return carry

    band_col = lax.broadcasted_iota(jnp.int32, (CHUNK, band), 1)

    def softmax(c, carry):
        outside = band_col < jnp.maximum(pad_chunks * CHUNK, front - c * CHUNK)
        for g in range(group):
            rows = slice(g * CHUNK, (g + 1) * CHUNK)
            sink = sink_ref[kvh * group + g]
            s = jnp.where(outside, F32(-1e30), s_ref[c, rows, :])
            m = jnp.maximum(jnp.max(s, axis=-1, keepdims=True), sink)
            p = jnp.exp(s - m)
            denom = jnp.sum(p, axis=-1, keepdims=True) + jnp.exp(sink - m)
            p_ref[c, rows, :] = (p / denom).astype(BF16)
        return carry

    def values(c, carry):
        o = jnp.dot(p_ref[c], vb_ref[band_rows(c), :], preferred_element_type=F32)
        for g in range(group):
            o_ref[chunk_rows(c), g * hd:(g + 1) * hd] = o[g * CHUNK:(g + 1) * CHUNK].astype(o_ref.dtype)
        return carry

    lax.fori_loop(0, n_chunks, scores, 0, unroll=2 * unroll)
    lax.fori_loop(0, n_chunks, softmax, 0, unroll=unroll)
    lax.fori_loop(0, n_chunks, values, 0, unroll=2 * unroll)


def _swa_sample_kernel(sink_ref, q_ref, k_ref, v_ref, ck_ref, cv_ref, o_in_ref, o_ref, kn_ref, vn_ref, *,
                       group, hd):
    del o_in_ref
    rows = q_ref.shape[0]
    kn_ref[...] = k_ref[...]
    vn_ref[...] = v_ref[...]
    for kv in range(k_ref.shape[1] // hd):
        kv_cols = slice(kv * hd, (kv + 1) * hd)
        kb = jnp.concatenate([ck_ref[:, kv_cols], k_ref[:, kv_cols]], axis=0).astype(BF16)
        vb = jnp.concatenate([cv_ref[:, kv_cols], v_ref[:, kv_cols]], axis=0).astype(BF16)
        sink_col = _sink_column(sink_ref, kv, group, rows)
        o = _swa_chunk(q_ref[:, kv * group * hd:(kv + 1) * group * hd], kb, vb, sink_col, group)
        for g in range(group):
            col0 = (kv * group + g) * hd
            o_ref[:, col0:col0 + hd] = o[g * rows:(g + 1) * rows].astype(o_ref.dtype)


def swa(z32, zb, sinks, cache_k, cache_v, layer, cfg):
    hd, kvh, group = cfg["head_dim"], cfg["kv_heads"], cfg["att_group"]
    b_p, t_p, b_s, t_s = cfg["b_p"], cfg["t_p"], cfg["b_s"], cfg["t_s"]
    rows = z32.shape[0]
    qw = group * hd
    assert cfg["off_q_a"] == 0
    k_blk0 = cfg["off_k_a"] // hd
    v_blk0 = cfg["off_v_a"] // hd
    att_w = cfg["att_width"]
    window = cache_k.shape[2]
    window_chunks = window // CHUNK
    smem = pl.BlockSpec(memory_space=pltpu.SMEM)

    lane_chunks = LANES // CHUNK
    pad_chunks = -(window_chunks + 1) % lane_chunks
    band = (window_chunks + pad_chunks + 1) * CHUNK
    front = band - CHUNK
    n_chunks = t_p // CHUNK
    need_p = (2 * (t_p * qw * 2 + 2 * t_p * hd * 4 + t_p * qw * 2) + 2 * (front + t_p) * hd * 2
              + n_chunks * group * CHUNK * band * 6 + (4 << 20))
    kv_w = kvh * hd
    o, kc_p, vc_p = pl.pallas_call(
        functools.partial(_swa_prompt_kernel, group=group, window_chunks=window_chunks, pad_chunks=pad_chunks,
                          unroll=2),
        out_shape=(jax.ShapeDtypeStruct((rows, att_w), BF16),
                   jax.ShapeDtypeStruct((b_p, window, kv_w), F32),
                   jax.ShapeDtypeStruct((b_p, window, kv_w), F32)),
        grid=(b_p, kvh),
        in_specs=[smem,
                  pl.BlockSpec((t_p, qw), lambda b, k: (b, k)),
                  pl.BlockSpec((t_p, hd), lambda b, k: (b, k_blk0 + k)),
                  pl.BlockSpec((t_p, hd), lambda b, k: (b, v_blk0 + k))],
        out_specs=(pl.BlockSpec((t_p, qw), lambda b, k: (b, k)),
                   pl.BlockSpec((None, window, hd), lambda b, k: (b, 0, k)),
                   pl.BlockSpec((None, window, hd), lambda b, k: (b, 0, k))),
        scratch_shapes=[pltpu.VMEM((front + t_p, hd), BF16), pltpu.VMEM((front + t_p, hd), BF16),
                        pltpu.VMEM((n_chunks, group * CHUNK, band), F32),
                        pltpu.VMEM((n_chunks, group * CHUNK, band), BF16)],
        compiler_params=_params(("parallel", "parallel"), need_p),
        name="swa_prompt",
    )(sinks, zb, z32, z32)

    assert t_s == CHUNK and window + t_s == (window_chunks + 1) * CHUNK
    rb0 = (b_p * t_p) // t_s
    assert cfg["off_k_a"] % kv_w == 0 and cfg["off_v_a"] % kv_w == 0
    o, kn_s, vn_s = pl.pallas_call(
        functools.partial(_swa_sample_kernel, group=group, hd=hd),
        out_shape=(jax.ShapeDtypeStruct((rows, att_w), BF16),
                   jax.ShapeDtypeStruct((b_s, t_s, kv_w), F32),
                   jax.ShapeDtypeStruct((b_s, t_s, kv_w), F32)),
        grid=(b_s,),
        in_specs=[smem,
                  pl.BlockSpec((t_s, att_w), lambda b: (rb0 + b, 0)),
                  pl.BlockSpec((t_s, kv_w), lambda b: (rb0 + b, cfg["off_k_a"] // kv_w)),
                  pl.BlockSpec((t_s, kv_w), lambda b: (rb0 + b, cfg["off_v_a"] // kv_w)),
                  pl.BlockSpec((None, None, window, kv_w), lambda b: (layer, b, 0, 0)),
                  pl.BlockSpec((None, None, window, kv_w), lambda b: (layer, b, 0, 0)),
                  pl.BlockSpec(memory_space=pl.ANY)],
        out_specs=(pl.BlockSpec((t_s, att_w), lambda b: (rb0 + b, 0)),
                   pl.BlockSpec((None, t_s, kv_w), lambda b: (b, 0, 0)),
                   pl.BlockSpec((None, t_s, kv_w), lambda b: (b, 0, 0))),
        input_output_aliases={6: 0},
        compiler_params=_params(("parallel",), 8 << 20),
        name="swa_sample",
    )(sinks, zb, z32, z32, cache_k, cache_v, o)
    return o, (kc_p, vc_p, kn_s, vn_s)


def _retention_state_update(k32, v_bf, zeta):
    kz_t = jnp.transpose(k32 * zeta).astype(BF16)
    return jnp.dot(kz_t, v_bf, preferred_element_type=F32)


def _retention_readout(q_bf, k_bf, v_bf, g32, state_bf, decay, xi, gain):
    s = lax.dot_general(q_bf, k_bf, (((1,), (1,)), ((), ())), preferred_element_type=F32) * decay
    o = (jnp.dot(s.astype(BF16), v_bf, preferred_element_type=F32)
         + jnp.dot(q_bf, state_bf, preferred_element_type=F32) * xi)
    mu = jnp.mean(o, axis=-1, keepdims=True)
    var = jnp.mean(jnp.square(o - mu), axis=-1, keepdims=True)
    on = (o - mu) * lax.rsqrt(var + GN_EPS) * gain
    return (g32 * jax.nn.sigmoid(g32)) * on


def _retention_prompt_kernel(cd_ref, q_ref, k_ref, v_ref, g_ref, decay_ref, xi_ref, zeta_ref, gain_ref,
                             o_ref, st_ref, sb_ref, *, block, unroll):
    h = pl.program_id(1)
    n_blocks = q_ref.shape[0] // block
    block_decay = cd_ref[h]
    zeta = zeta_ref[...]

    state = jnp.zeros(st_ref.shape, F32)
    for c in range(n_blocks):
        rows = slice(c * block, (c + 1) * block)
        sb_ref[c] = state.astype(BF16)
        state = block_decay * state + _retention_state_update(k_ref[rows, :], v_ref[rows, :].astype(BF16), zeta)
    st_ref[...] = state

    gain = gain_ref[pl.ds(h, 1), :]

    def body(t, carry):
        for u in range(unroll):
            c = t * unroll + u
            rows = pl.ds(pl.multiple_of(c * block, block), block)
            out = _retention_readout(q_ref[rows, :].astype(BF16), k_ref[rows, :].astype(BF16),
                                     v_ref[rows, :].astype(BF16), g_ref[rows, :].astype(F32), sb_ref[c],
                                     decay_ref[...], xi_ref[...], gain)
            o_ref[rows, :] = out.astype(o_ref.dtype)
        return carry

    lax.fori_loop(0, n_blocks // unroll, body, 0)


def _retention_sample_kernel(cd_ref, q_ref, k_ref, v_ref, g_ref, decay_ref, xi_ref, zeta_ref, gain_ref,
                             st_in_ref, o_in_ref, o_ref, st_ref):
    del o_in_ref
    n_heads, dk, dv = st_in_ref.shape
    head0 = pl.program_id(1) * n_heads
    for i in range(n_heads):
        qk_cols, v_cols = slice(i * dk, (i + 1) * dk), slice(i * dv, (i + 1) * dv)
        state = st_in_ref[i]
        k32 = k_ref[:, qk_cols]
        v_bf = v_ref[:, v_cols].astype(BF16)
        out = _retention_readout(q_ref[:, qk_cols].astype(BF16), k32.astype(BF16), v_bf, g_ref[:, v_cols].astype(F32),
                                 state.astype(BF16), decay_ref[i], xi_ref[i], gain_ref[pl.ds(head0 + i, 1), :])
        o_ref[:, v_cols] = out.astype(o_ref.dtype)
        st_ref[i] = cd_ref[head0 + i] * state + _retention_state_update(k32, v_bf, zeta_ref[i])


def retention(z32, zb, state_in, gain, layer, consts_p, consts_s, cfg):
    dk, dv, heads = cfg["ret_dk"], cfg["ret_dv"], cfg["ret_heads"]
    b_p, t_p, b_s, t_s = cfg["b_p"], cfg["t_p"], cfg["b_s"], cfg["t_s"]
    rows = z32.shape[0]
    smem = pl.BlockSpec(memory_space=pltpu.SMEM)
    col_offsets = (("off_q_r", dk), ("off_k_r", dk), ("off_v_rb", dv), ("off_g_rb", dv))

    def specs(t, length, hg, row_block0):
        lead, hg = hg, hg or 1
        col0 = [cfg[o] // (hg * w) for o, w in col_offsets]
        return ([pl.BlockSpec((t, hg * w), lambda b, h, c0=c0: (row_block0 + b, c0 + h))
                 for c0, (_, w) in zip(col0, col_offsets)] +
                [pl.BlockSpec((lead, length, length), lambda b, h: (h, 0, 0)),
                 pl.BlockSpec((lead, length, 1), lambda b, h: (h, 0, 0)),
                 pl.BlockSpec((lead, length, 1), lambda b, h: (h, 0, 0)),
                 pl.BlockSpec((heads, dv), lambda b, h: (0, 0))])

    decay, xi, zeta, block_decay = consts_p
    block = decay.shape[1]
    n_blocks = t_p // block
    unroll = max(u for u in (1, 2, 4, 8) if n_blocks % u == 0)
    need_p = (2 * (2 * t_p * dk * 4 + 2 * t_p * dv * 4 + t_p * dv * 2 + block * block * 4)
              + n_blocks * dk * dv * 2 + (8 << 20))
    rr, st_p = pl.pallas_call(
        functools.partial(_retention_prompt_kernel, block=block, unroll=unroll),
        out_shape=(jax.ShapeDtypeStruct((rows, heads * dv), BF16),
                   jax.ShapeDtypeStruct((b_p, heads, dk, dv), F32)),
        grid=(b_p, heads),
        in_specs=[smem] + specs(t_p, block, None, 0),
        out_specs=(pl.BlockSpec((t_p, dv), lambda b, h: (b, h)),
                   pl.BlockSpec((None, None, dk, dv), lambda b, h: (b, h, 0, 0))),
        scratch_shapes=[pltpu.VMEM((n_blocks, dk, dv), BF16)],
        compiler_params=_params(("parallel", "parallel"), need_p),
        name="retention_prompt",
    )(block_decay, zb, z32, zb, zb, decay, xi, zeta, gain)

    decay, xi, zeta, block_decay = consts_s
    assert decay.shape[1] == t_s
    rb0 = (b_p * t_p) // t_s
    hg = max(n for n in range(1, heads + 1)
             if heads % n == 0 and all(cfg[o] % (n * w) == 0 for o, w in col_offsets))
    rr, st_s = pl.pallas_call(
        _retention_sample_kernel,
        out_shape=(jax.ShapeDtypeStruct((rows, heads * dv), BF16),
                   jax.ShapeDtypeStruct((b_s, heads, dk, dv), F32)),
        grid=(b_s, heads // hg),
        in_specs=[smem] + specs(t_s, t_s, hg, rb0) +
                 [pl.BlockSpec((None, None, hg, dk, dv), lambda b, h: (layer, b, h, 0, 0)),
                  pl.BlockSpec(memory_space=pl.ANY)],
        out_specs=(pl.BlockSpec((t_s, hg * dv), lambda b, h: (rb0 + b, h)),
                   pl.BlockSpec((None, hg, dk, dv), lambda b, h: (b, h, 0, 0))),
        input_output_aliases={10: 0},
        compiler_params=_params(("parallel", "parallel"), 16 << 20),
        name="retention_sample",
    )(block_decay, zb, z32, zb, zb, decay, xi, zeta, gain, state_in, rr)
    return rr, st_p, st_s


def _merge_kernel(*refs, parts):
    (oa_ref, rr_ref, wa_ref, wr_ref, ga_ref, gr_ref), (o_ref,), cast_src, cast_dst, _ = _split_refs(refs, 6, 1)
    for rows in _row_parts(o_ref, parts):
        branch_a = jnp.dot(oa_ref[rows, :], wa_ref[...], preferred_element_type=F32)
        branch_r = jnp.dot(rr_ref[rows, :], wr_ref[...], preferred_element_type=F32)
        merged = (jax.nn.sigmoid(ga_ref[rows, :].astype(F32)) * branch_a
                  + jax.nn.sigmoid(gr_ref[rows, :].astype(F32)) * branch_r)
        o_ref[rows, :] = merged.astype(o_ref.dtype)
    _run_side_casts(cast_src, cast_dst)


def merge_branches(o_a, rr, w_a, w_r, zb, cfg, cast_jobs=()):
    rows, ka = o_a.shape
    kr = rr.shape[1]
    d = w_a.shape[1]
    tm = _pick(rows, 768, 8)
    tn = _pick(d, 1024, LANES)
    ni, nj = rows // tm, d // tn
    casts = [SideCast(st, layer, ni * nj, nj) for st, layer in cast_jobs]
    ga0, gr0 = cfg["off_gate_ab"] // tn, cfg["off_gate_rb"] // tn
    assert cfg["off_gate_ab"] % tn == 0 and cfg["off_gate_rb"] % tn == 0
    need = (2 * (tm * (ka + kr) * 2 + (ka + kr) * tn * 2 + 3 * tm * tn * 2) + 4 * (tm // 4) * tn * 4
            + sum(c.vmem for c in casts))
    out = pl.pallas_call(
        functools.partial(_merge_kernel, parts=4 if tm % 64 == 0 else 1),
        out_shape=[jax.ShapeDtypeStruct((rows, d), BF16)] + [c.out_shape for c in casts],
        grid=(ni, nj),
        in_specs=[pl.BlockSpec((tm, ka), lambda i, j: (i, 0)),
                  pl.BlockSpec((tm, kr), lambda i, j: (i, 0)),
                  pl.BlockSpec((ka, tn), lambda i, j: (0, j)),
                  pl.BlockSpec((kr, tn), lambda i, j: (0, j)),
                  pl.BlockSpec((tm, tn), lambda i, j: (i, ga0 + j)),
                  pl.BlockSpec((tm, tn), lambda i, j: (i, gr0 + j))] + [c.in_spec for c in casts],
        out_specs=[pl.BlockSpec((tm, tn), lambda i, j: (i, j))] + [c.out_spec for c in casts],
        compiler_params=_params(("arbitrary", "arbitrary"), need),
        name="merge_branches",
    )(o_a, rr, w_a, w_r, zb, zb, *[c.operand for c in casts])
    return out[0], out[1:]


def _matmul_residual_kernel(*refs, parts, fold_norm):
    ins, outs, cast_src, cast_dst, _ = _split_refs(refs, 4 if fold_norm else 3, 3 if fold_norm else 1)
    a_ref, w_ref, x_ref = ins[:3]
    o_ref = outs[0]
    if fold_norm:
        g_ref, (xg_ref, ssq_ref) = ins[3], outs[1:]

        @pl.when(pl.program_id(1) == 0)
        def _():
            ssq_ref[...] = jnp.zeros(ssq_ref.shape, F32)

    for rows in _row_parts(o_ref, parts):
        y = x_ref[rows, :] + jnp.dot(a_ref[rows, :], w_ref[...], preferred_element_type=F32)
        o_ref[rows, :] = y
        if fold_norm:
            xg_ref[rows, :] = (y * g_ref[...]).astype(BF16)
            ssq_ref[rows, :] += _lane_group_sum(y * y)
    _run_side_casts(cast_src, cast_dst)


def matmul_residual(a, w, x, tm_target, tn_target, next_gain=None, cast_jobs=()):
    rows, k = a.shape
    n = w.shape[1]
    tm = _pick(rows, tm_target, 8)
    tn = _pick(n, tn_target, LANES)
    ni, nj = rows // tm, n // tn
    fold_norm = next_gain is not None
    casts = [SideCast(st, layer, ni * nj, nj) for st, layer in cast_jobs]
    need = (2 * (tm * k * 2 + k * tn * 2 + 2 * tm * tn * 4 + tm * tn * 2 + tm * LANES * 4) + 3 * tm * tn * 4
            + sum(c.vmem for c in casts))
    tile = pl.BlockSpec((tm, tn), lambda i, j: (i, j))
    in_specs = [pl.BlockSpec((tm, k), lambda i, j: (i, 0)), pl.BlockSpec((k, tn), lambda i, j: (0, j)), tile]
    out_specs, out_shape, operands = [tile], [jax.ShapeDtypeStruct((rows, n), F32)], [a, w, x]
    if fold_norm:
        in_specs.append(pl.BlockSpec((1, tn), lambda i, j: (0, j)))
        operands.append(next_gain.reshape(1, n))
        out_specs += [tile, pl.BlockSpec((tm, LANES), lambda i, j: (i, 0))]
        out_shape += [jax.ShapeDtypeStruct((rows, n), BF16), jax.ShapeDtypeStruct((rows, LANES), F32)]
    out = pl.pallas_call(
        functools.partial(_matmul_residual_kernel, parts=4 if tm % 64 == 0 else 1, fold_norm=fold_norm),
        out_shape=out_shape + [c.out_shape for c in casts],
        grid=(ni, nj),
        in_specs=in_specs + [c.in_spec for c in casts],
        out_specs=out_specs + [c.out_spec for c in casts],
        compiler_params=_params(("arbitrary", "arbitrary"), need),
        name="matmul_residual",
    )(*operands, *[c.operand for c in casts])
    n_own = len(out_shape)
    return out[:n_own], out[n_own:]


def _gate_up_kernel(*refs, parts, d_model):
    (xg_ref, wg_ref, wu_ref, ssq_ref), (o_ref,), cast_src, cast_dst, (rinv_ref,) = _split_refs(refs, 4, 1, 1)
    _store_row_scale(ssq_ref, rinv_ref, d_model)
    for rows in _row_parts(o_ref, parts):
        xg = xg_ref[rows, :]
        rinv = rinv_ref[rows, :]
        gate_acc = jnp.dot(xg, wg_ref[...], preferred_element_type=F32)
        up_acc = jnp.dot(xg, wu_ref[...], preferred_element_type=F32)
        for g in range(o_ref.shape[1] // LANES):
            cols = slice(g * LANES, (g + 1) * LANES)
            gate = gate_acc[:, cols] * rinv
            up = up_acc[:, cols] * rinv
            o_ref[rows, cols] = ((gate * jax.nn.sigmoid(gate)) * up).astype(o_ref.dtype)
    _run_side_casts(cast_src, cast_dst)


def gate_up(xg, ssq, w_gate_up, cast_jobs=()):
    rows, d = xg.shape
    f = w_gate_up.shape[1] // 2
    tm = _pick(rows, 1536, 8)
    tn = _pick(f, 256, LANES)
    ni, nj = rows // tm, f // tn
    casts = [SideCast(st, layer, ni * nj, nj) for st, layer in cast_jobs]
    need = (2 * (tm * d * 2 + 2 * d * tn * 2 + tm * tn * 2 + tm * LANES * 4) + 4 * tm * tn * 4 + tm * LANES * 4
            + sum(c.vmem for c in casts))
    out = pl.pallas_call(
        functools.partial(_gate_up_kernel, parts=2 if tm % 16 == 0 else 1, d_model=d),
        out_shape=[jax.ShapeDtypeStruct((rows, f), BF16)] + [c.out_shape for c in casts],
        grid=(ni, nj),
        in_specs=[pl.BlockSpec((tm, d), lambda i, j: (i, 0)),
                  pl.BlockSpec((d, tn), lambda i, j: (0, j)),
                  pl.BlockSpec((d, tn), lambda i, j: (0, nj + j)),
                  pl.BlockSpec((tm, LANES), lambda i, j: (i, 0))] + [c.in_spec for c in casts],
        out_specs=[pl.BlockSpec((tm, tn), lambda i, j: (i, j))] + [c.out_spec for c in casts],
        scratch_shapes=[pltpu.VMEM((tm, LANES), F32)],
        compiler_params=_params(("arbitrary", "arbitrary"), need),
        name="swiglu_gate_up",
    )(xg, w_gate_up, w_gate_up, ssq, *[c.operand for c in casts])
    return out[0], out[1:]


def _rope_tables(positions, hd):
    half = hd // 2
    inv = ROPE_THETA ** (-jnp.arange(half, dtype=F32) / half)
    ang = positions.astype(F32)[:, None] * inv[None, :]
    cos, sin = jnp.cos(ang), jnp.sin(ang)
    return jnp.concatenate([cos, cos, -sin, sin], axis=-1)


def _retention_tables(heads, length):
    lg = jnp.log1p(-jnp.exp2(-5.0 - jnp.arange(heads, dtype=F32)))
    j = jnp.arange(length, dtype=F32)
    rel = j[:, None] - j[None, :]
    decay = jnp.where(rel >= 0, jnp.exp(jnp.maximum(rel, 0.0)[None] * lg[:, None, None]), 0.0)
    xi = jnp.exp((j + 1.0)[None, :] * lg[:, None])[:, :, None]
    zeta = jnp.exp((length - 1.0 - j)[None, :] * lg[:, None])[:, :, None]
    block_decay = jnp.exp(length * lg)
    return decay, xi, zeta, block_decay


def kernel(x_prompt, x_sample, cache_swa_k, cache_swa_v, state_retention, w_in, w_proj_a, w_proj_r, w_out,
           attn_sinks, ret_norm_gain, norm_mix, norm_ffn, w_gate_up, w_down, norm_final):
    b_p, t_p, d = x_prompt.shape
    b_s, t_s, _ = x_sample.shape
    depth = w_in.shape[0]
    _, _, window, kvh, hd = cache_swa_k.shape
    att_heads = attn_sinks.shape[1]
    _, _, ret_heads, ret_dk, ret_dv = state_retention.shape
    att_w, kv_w = att_heads * hd, kvh * hd
    rqk_w, rv_w = ret_heads * ret_dk, ret_heads * ret_dv
    splits = (att_w, kv_w, kv_w, rqk_w, rqk_w, rv_w, rv_w, d, d)
    offs = [0]
    for s in splits:
        offs.append(offs[-1] + s)
    assert offs[-1] == w_in.shape[2] and hd == LANES and ret_dk == LANES
    split_kind = (OP_ROPE_SCALED + DST_BF16, OP_ROPE, OP_PLAIN, OP_ROPE + DST_BF16, OP_ROPE_SCALED,
                  OP_PLAIN + DST_BF16, OP_PLAIN + DST_BF16, OP_PLAIN + DST_BF16, OP_PLAIN + DST_BF16)
    assert hd == ret_dk
    tn_in = _pick(kv_w, 512, LANES)
    assert all(o % tn_in == 0 for o in offs)
    tile_kinds = [kd for kd, s in zip(split_kind, splits) for _ in range(s // tn_in)]
    off_in, fill = [], {False: 0, True: 0}
    for kd, s in zip(split_kind, splits):
        off_in.append(fill[kd >= DST_BF16])
        fill[kd >= DST_BF16] += s
    cfg = dict(b_p=b_p, t_p=t_p, b_s=b_s, t_s=t_s, head_dim=hd, kv_heads=kvh, att_group=att_heads // kvh,
               att_width=att_w, ret_heads=ret_heads, ret_dk=ret_dk, ret_dv=ret_dv,
               off_q_a=off_in[0], off_k_a=off_in[1], off_v_a=off_in[2], off_q_r=off_in[3], off_k_r=off_in[4],
               off_v_rb=off_in[5], off_g_rb=off_in[6], off_gate_ab=off_in[7], off_gate_rb=off_in[8])

    pos = jnp.concatenate([jnp.tile(jnp.arange(t_p, dtype=jnp.int32), b_p),
                           jnp.tile(PAST_LEN + jnp.arange(t_s, dtype=jnp.int32), b_s)])
    rope_t = _rope_tables(pos, hd)
    ret_consts_p = _retention_tables(ret_heads, _pick(t_p, RET_BLOCK_TARGET, CHUNK))
    ret_consts_s = _retention_tables(ret_heads, t_s)

    cache_k = cache_swa_k.reshape(depth, b_s, window, kv_w)
    cache_v = cache_swa_v.reshape(depth, b_s, window, kv_w)
    w_in_b, w_a_b, w_r_b, w_out_b, w_down_b = (w[0].astype(BF16) for w in (w_in, w_proj_a, w_proj_r, w_out, w_down))

    rows_p = b_p * t_p
    x, xg, ssq = norm_prep(x_prompt.reshape(rows_p, d), x_sample.reshape(b_s * t_s, d), norm_mix[0])

    kp_rows, vp_rows, sp_states, ks_rows, vs_rows, ss_states = [], [], [], [], [], []
    for layer in range(depth):
        nxt = layer + 1

        def next_layer(*stacked):
            return [(w, nxt) for w in stacked] if nxt < depth else []

        z32, zb, (w_gu_b,) = inproj(xg, ssq, w_in_b, tile_kinds, rope_t, tn_in, hd ** -0.5, [(w_gate_up, layer)])
        o_a, (kc_p, vc_p, kn_s, vn_s) = swa(z32, zb, attn_sinks[layer], cache_k, cache_v, layer, cfg)
        rr, st_p, st_s = retention(z32, zb, state_retention, ret_norm_gain[layer], layer,
                                   ret_consts_p, ret_consts_s, cfg)
        merged, _ = merge_branches(o_a, rr, w_a_b, w_r_b, zb, cfg)
        (x, xg, ssq), next_proj = matmul_residual(merged, w_out_b, x, 768, 1024, norm_ffn[layer],
                                                  next_layer(w_out, w_proj_a, w_proj_r))
        act, next_in = gate_up(xg, ssq, w_gu_b, next_layer(w_in))
        if nxt < depth:
            (x, xg, ssq), next_down = matmul_residual(act, w_down_b, x, 768, 256, norm_mix[nxt], next_layer(w_down))
            (w_out_b, w_a_b, w_r_b), (w_in_b,), (w_down_b,) = next_proj, next_in, next_down
        else:
            (x,), _ = matmul_residual(act, w_down_b, x, 768, 256)

        kp_rows.append(kc_p.reshape(b_p, window, kvh, hd))
        vp_rows.append(vc_p.reshape(b_p, window, kvh, hd))
        ks_rows.append(kn_s.reshape(b_s, t_s, kvh, hd))
        vs_rows.append(vn_s.reshape(b_s, t_s, kvh, hd))
        sp_states.append(st_p)
        ss_states.append(st_s)

    y_p, y_s = rmsnorm_split(x, norm_final, rows_p)
    return (y_p.reshape(b_p, t_p, d), y_s.reshape(b_s, t_s, d),
            jnp.stack(kp_rows), jnp.stack(vp_rows), jnp.stack(sp_states),
            jnp.stack(ks_rows), jnp.stack(vs_rows), jnp.stack(ss_states))
```

```python
import functools
import math

import jax
import jax.numpy as jnp
from jax import lax
from jax.experimental import pallas as pl
from jax.experimental.pallas import tpu as pltpu

F32 = jnp.float32
BF16 = jnp.bfloat16

CHUNK = 64
PAST_LEN = 2048
ROPE_THETA = 10000.0
RMS_EPS = 1e-6
GN_EPS = 1e-5

LANES = 128
SUBLANES_BF16 = 16
VMEM_BYTES_V7X = 64 * 1024 * 1024
VMEM_BUDGET = VMEM_BYTES_V7X - 8 * 1024 * 1024

RET_BLOCK_TARGET = 256


def _pick(total, target, mult):
    best = None
    for d in range(mult, min(total, target) + 1, mult):
        if total % d == 0:
            best = d
    if best is None:
        raise ValueError(f"no tile for {total} (target {target}, multiple of {mult})")
    return best


def _params(semantics, vmem_need):
    limit = min(max(int(vmem_need * 1.2), 16 * 1024 * 1024), VMEM_BUDGET)
    return pltpu.CompilerParams(dimension_semantics=semantics, vmem_limit_bytes=limit)


def _row_parts(ref, parts):
    rows = ref.shape[0]
    sub = rows // parts
    return [slice(p * sub, (p + 1) * sub) for p in range(parts)]


class SideCast:
    def __init__(self, stacked, layer, n_steps, nj):
        _, k, n = stacked.shape
        rows = min(r for r in range(SUBLANES_BF16, k + 1, SUBLANES_BF16) if k % r == 0 and k // r <= n_steps)
        last = k // rows - 1
        self.operand = stacked
        self.out_shape = jax.ShapeDtypeStruct((k, n), BF16)
        self.in_spec = pl.BlockSpec((None, rows, n), lambda i, j, *_: (layer, jnp.minimum(i * nj + j, last), 0))
        self.out_spec = pl.BlockSpec((rows, n), lambda i, j, *_: (jnp.minimum(i * nj + j, last), 0))
        self.vmem = 2 * rows * n * 6


def _run_side_casts(src_refs, dst_refs):
    for src, dst in zip(src_refs, dst_refs):
        dst[...] = src[...].astype(BF16)


def _split_refs(refs, n_in, n_out, n_scratch=0):
    n_cast = (len(refs) - n_in - n_out - n_scratch) // 2
    a, b, c = n_in + n_cast, n_in + n_cast + n_out, n_in + 2 * n_cast + n_out
    return refs[:n_in], refs[a:b], refs[n_in:a], refs[b:c], refs[c:]


def _lane_group_sum(x):
    acc = x[:, :LANES]
    for t in range(1, x.shape[1] // LANES):
        acc = acc + x[:, t * LANES:(t + 1) * LANES]
    return acc


def _store_row_scale(ssq_ref, scale_ref, d_model):
    @pl.when(pl.program_id(1) == 0)
    def _():
        mean_sq = jnp.sum(ssq_ref[...], axis=-1, keepdims=True) / d_model
        scale_ref[...] = jnp.broadcast_to(lax.rsqrt(mean_sq + RMS_EPS), scale_ref.shape)


def _norm_prep_kernel(xp_ref, xs_ref, g_ref, x_ref, xg_ref, ssq_ref, *, head_blocks):
    def emit(src_ref):
        x = src_ref[...]
        x_ref[...] = x
        xg_ref[...] = (x * g_ref[...]).astype(BF16)
        ssq_ref[...] = _lane_group_sum(x * x)

    i = pl.program_id(0)
    pl.when(i < head_blocks)(functools.partial(emit, xp_ref))
    pl.when(i >= head_blocks)(functools.partial(emit, xs_ref))


def norm_prep(x_head, x_tail, g):
    head_rows, d = x_head.shape
    tail_rows = x_tail.shape[0]
    rows = head_rows + tail_rows
    tr = _pick(math.gcd(head_rows, tail_rows), 256, 8)
    hb = head_rows // tr
    return pl.pallas_call(
        functools.partial(_norm_prep_kernel, head_blocks=hb),
        out_shape=(jax.ShapeDtypeStruct((rows, d), F32), jax.ShapeDtypeStruct((rows, d), BF16),
                   jax.ShapeDtypeStruct((rows, LANES), F32)),
        grid=(rows // tr,),
        in_specs=[pl.BlockSpec((tr, d), lambda i: (jnp.minimum(i, hb - 1), 0)),
                  pl.BlockSpec((tr, d), lambda i: (jnp.maximum(i - hb, 0), 0)),
                  pl.BlockSpec((1, d), lambda i: (0, 0))],
        out_specs=(pl.BlockSpec((tr, d), lambda i: (i, 0)),
                   pl.BlockSpec((tr, d), lambda i: (i, 0)),
                   pl.BlockSpec((tr, LANES), lambda i: (i, 0))),
        compiler_params=_params(("arbitrary",), 16 * tr * d * 4),
        name="norm_prep",
    )(x_head, x_tail, g.reshape(1, d))


def _rmsnorm_split_kernel(x_ref, g_ref, head_ref, tail_ref, *, head_blocks):
    x = x_ref[...]
    y = x * lax.rsqrt(jnp.mean(x * x, axis=-1, keepdims=True) + RMS_EPS) * g_ref[...]
    i = pl.program_id(0)

    @pl.when(i < head_blocks)
    def _():
        head_ref[...] = y

    @pl.when(i >= head_blocks)
    def _():
        tail_ref[...] = y


def rmsnorm_split(x, g, head_rows):
    rows, d = x.shape
    assert 0 < head_rows < rows
    tr = _pick(math.gcd(head_rows, rows - head_rows), 256, 8)
    hb = head_rows // tr
    return pl.pallas_call(
        functools.partial(_rmsnorm_split_kernel, head_blocks=hb),
        out_shape=(jax.ShapeDtypeStruct((head_rows, d), F32), jax.ShapeDtypeStruct((rows - head_rows, d), F32)),
        grid=(rows // tr,),
        in_specs=[pl.BlockSpec((tr, d), lambda i: (i, 0)),
                  pl.BlockSpec((1, d), lambda i: (0, 0))],
        out_specs=(pl.BlockSpec((tr, d), lambda i: (jnp.minimum(i, hb - 1), 0)),
                   pl.BlockSpec((tr, d), lambda i: (jnp.maximum(i - hb, 0), 0))),
        compiler_params=_params(("arbitrary",), 10 * tr * d * 4),
        name="rmsnorm_split",
    )(x, g.reshape(1, d))


OP_PLAIN, OP_ROPE, OP_ROPE_SCALED = 0, 1, 2
DST_BF16 = 4


def _inproj_kernel(kind_ref, f32_blk_ref, bf16_blk_ref, *refs, scale, parts, d_model):
    del f32_blk_ref, bf16_blk_ref
    ((xg_ref, w_ref, rope_ref, ssq_ref), (o_ref, ob_ref), cast_src, cast_dst,
     (rinv_ref,)) = _split_refs(refs, 4, 2, 1)
    kind = kind_ref[pl.program_id(1)]
    _store_row_scale(ssq_ref, rinv_ref, d_model)
    groups = [slice(g * LANES, (g + 1) * LANES) for g in range(o_ref.shape[1] // LANES)]

    def plain(dst_ref):
        for rows in _row_parts(dst_ref, parts):
            acc = jnp.dot(xg_ref[rows, :], w_ref[...], preferred_element_type=F32)
            rinv = rinv_ref[rows, :]
            for cols in groups:
                dst_ref[rows, cols] = (acc[:, cols] * rinv).astype(dst_ref.dtype)
        _run_side_casts(cast_src, cast_dst)

    def rotary(dst_ref):
        s = jnp.where(kind % DST_BF16 == OP_ROPE_SCALED, F32(scale), F32(1.0))
        for rows in _row_parts(dst_ref, parts):
            acc = jnp.dot(xg_ref[rows, :], w_ref[...], preferred_element_type=F32)
            rinv = rinv_ref[rows, :]
            c = rope_ref[rows, :LANES]
            sn = rope_ref[rows, LANES:]
            for cols in groups:
                xn = acc[:, cols] * rinv
                r = xn * c + pltpu.roll(xn, LANES // 2, 1) * sn
                dst_ref[rows, cols] = (r * s).astype(dst_ref.dtype)
        _run_side_casts(cast_src, cast_dst)

    pl.when(kind == OP_PLAIN)(functools.partial(plain, o_ref))
    pl.when(kind == OP_PLAIN + DST_BF16)(functools.partial(plain, ob_ref))
    pl.when((kind == OP_ROPE) | (kind == OP_ROPE_SCALED))(functools.partial(rotary, o_ref))
    pl.when((kind == OP_ROPE + DST_BF16) | (kind == OP_ROPE_SCALED + DST_BF16))(functools.partial(rotary, ob_ref))


def inproj(xg, ssq, w, tile_kinds, rope_t, tn, scale, cast_jobs=()):
    rows, d = xg.shape
    n = w.shape[1]
    tm = _pick(rows, 1536, 8)
    ni, nj = rows // tm, n // tn
    assert len(tile_kinds) == nj
    to_bf16 = [k >= DST_BF16 for k in tile_kinds]
    f32_blk = [max(sum(not t for t in to_bf16[:j + 1]) - 1, 0) for j in range(nj)]
    bf16_blk = [max(sum(to_bf16[:j + 1]) - 1, 0) for j in range(nj)]
    n_bf16 = sum(to_bf16) * tn
    casts = [SideCast(st, layer, ni * nj, nj) for st, layer in cast_jobs]
    need = (2 * (tm * d * 2 + d * tn * 2 + tm * tn * 6 + 3 * tm * LANES * 4) + 5 * tm * LANES * 4
            + sum(c.vmem for c in casts))
    grid_spec = pltpu.PrefetchScalarGridSpec(
        num_scalar_prefetch=3,
        grid=(ni, nj),
        in_specs=[pl.BlockSpec((tm, d), lambda i, j, *_: (i, 0)),
                  pl.BlockSpec((d, tn), lambda i, j, *_: (0, j)),
                  pl.BlockSpec((tm, 2 * LANES), lambda i, j, *_: (i, 0)),
                  pl.BlockSpec((tm, LANES), lambda i, j, *_: (i, 0))] + [c.in_spec for c in casts],
        out_specs=[pl.BlockSpec((tm, tn), lambda i, j, kinds, f32_b, bf16_b: (i, f32_b[j])),
                   pl.BlockSpec((tm, tn), lambda i, j, kinds, f32_b, bf16_b: (i, bf16_b[j]))]
                  + [c.out_spec for c in casts],
        scratch_shapes=[pltpu.VMEM((tm, LANES), F32)],
    )
    out = pl.pallas_call(
        functools.partial(_inproj_kernel, scale=scale, parts=4 if tm % 64 == 0 else 1, d_model=d),
        out_shape=[jax.ShapeDtypeStruct((rows, n - n_bf16), F32), jax.ShapeDtypeStruct((rows, n_bf16), BF16)]
                  + [c.out_shape for c in casts],
        grid_spec=grid_spec,
        compiler_params=_params(("arbitrary", "arbitrary"), need),
        name="inproj_rope",
    )(jnp.asarray(tile_kinds, jnp.int32), jnp.asarray(f32_blk, jnp.int32), jnp.asarray(bf16_blk, jnp.int32),
      xg, w, rope_t, ssq, *[c.operand for c in casts])
    return out[0], out[1], out[2:]


def _sink_column(sink_ref, kvh, group, rows_per_head):
    cols = [jnp.full((rows_per_head, 1), sink_ref[kvh * group + g], F32) for g in range(group)]
    return jnp.concatenate(cols, axis=0)


def _swa_chunk(q_rows, kb, vb, sink_col, group):
    hd = kb.shape[1]
    q = jnp.concatenate([q_rows[:, g * hd:(g + 1) * hd] for g in range(group)], axis=0).astype(BF16)
    s = lax.dot_general(q, kb, (((1,), (1,)), ((), ())), preferred_element_type=F32)
    m = jnp.maximum(jnp.max(s, axis=-1, keepdims=True), sink_col)
    p = jnp.exp(s - m)
    denom = jnp.sum(p, axis=-1, keepdims=True) + jnp.exp(sink_col - m)
    p = (p / denom).astype(BF16)
    return jnp.dot(p, vb, preferred_element_type=F32)


def _swa_prompt_kernel(sink_ref, q_ref, k_ref, v_ref, o_ref, kc_ref, vc_ref, kb_ref, vb_ref, s_ref, p_ref, *,
                       group, window_chunks, pad_chunks, unroll):
    seq, hd = k_ref.shape
    n_chunks = seq // CHUNK
    kvh = pl.program_id(1)
    front = (window_chunks + pad_chunks) * CHUNK
    band = front + CHUNK
    kb_ref[:front, :] = jnp.zeros((front, hd), BF16)
    vb_ref[:front, :] = jnp.zeros((front, hd), BF16)
    kb_ref[front:, :] = k_ref[...].astype(BF16)
    vb_ref[front:, :] = v_ref[...].astype(BF16)
    kc_ref[...] = k_ref[seq - kc_ref.shape[0]:, :]
    vc_ref[...] = v_ref[seq - vc_ref.shape[0]:, :]

    def chunk_rows(c):
        return pl.ds(pl.multiple_of(c * CHUNK, CHUNK), CHUNK)

    def band_rows(c):
        return pl.ds(pl.multiple_of(c * CHUNK, CHUNK), band)

    def scores(c, carry):
        q_rows = q_ref[chunk_rows(c), :]
        q = jnp.concatenate([q_rows[:, g * hd:(g + 1) * hd] for g in range(group)], axis=0).astype(BF16)
        s_ref[c] = lax.dot_general(q, kb_ref[band_rows(c), :], (((1,), (1,)), ((), ())),
                                   preferred_element_type=F32)
        return carry

    band_col = lax.broadcasted_iota(jnp.int32, (CHUNK, band), 1)

    def softmax(c, carry):
        outside = band_col < jnp.maximum(pad_chunks * CHUNK, front - c * CHUNK)
        for g in range(group):
            rows = slice(g * CHUNK, (g + 1) * CHUNK)
            sink = sink_ref[kvh * group + g]
            s = jnp.where(outside, F32(-1e30), s_ref[c, rows, :])
            m = jnp.maximum(jnp.max(s, axis=-1, keepdims=True), sink)
            p = jnp.exp(s - m)
            denom = jnp.sum(p, axis=-1, keepdims=True) + jnp.exp(sink - m)
            p_ref[c, rows, :] = (p / denom).astype(BF16)
        return carry

    def values(c, carry):
        o = jnp.dot(p_ref[c], vb_ref[band_rows(c), :], preferred_element_type=F32)
        for g in range(group):
            o_ref[chunk_rows(c), g * hd:(g + 1) * hd] = o[g * CHUNK:(g + 1) * CHUNK].astype(o_ref.dtype)
        return carry

    lax.fori_loop(0, n_chunks, scores, 0, unroll=4 * unroll)
    lax.fori_loop(0, n_chunks, softmax, 0, unroll=unroll)
    lax.fori_loop(0, n_chunks, values, 0, unroll=4 * unroll)


def _swa_sample_kernel(sink_ref, q_ref, k_ref, v_ref, ck_ref, cv_ref, o_in_ref, o_ref, kn_ref, vn_ref, *,
                       group, hd):
    del o_in_ref
    rows = q_ref.shape[0]
    kn_ref[...] = k_ref[...]
    vn_ref[...] = v_ref[...]
    for kv in range(k_ref.shape[1] // hd):
        kv_cols = slice(kv * hd, (kv + 1) * hd)
        kb = jnp.concatenate([ck_ref[:, kv_cols], k_ref[:, kv_cols]], axis=0).astype(BF16)
        vb = jnp.concatenate([cv_ref[:, kv_cols], v_ref[:, kv_cols]], axis=0).astype(BF16)
        sink_col = _sink_column(sink_ref, kv, group, rows)
        o = _swa_chunk(q_ref[:, kv * group * hd:(kv + 1) * group * hd], kb, vb, sink_col, group)
        for g in range(group):
            col0 = (kv * group + g) * hd
            o_ref[:, col0:col0 + hd] = o[g * rows:(g + 1) * rows].astype(o_ref.dtype)


def swa(z32, zb, sinks, cache_k, cache_v, layer, cfg):
    hd, kvh, group = cfg["head_dim"], cfg["kv_heads"], cfg["att_group"]
    b_p, t_p, b_s, t_s = cfg["b_p"], cfg["t_p"], cfg["b_s"], cfg["t_s"]
    rows = z32.shape[0]
    qw = group * hd
    assert cfg["off_q_a"] == 0
    k_blk0 = cfg["off_k_a"] // hd
    v_blk0 = cfg["off_v_a"] // hd
    att_w = cfg["att_width"]
    window = cache_k.shape[2]
    window_chunks = window // CHUNK
    smem = pl.BlockSpec(memory_space=pltpu.SMEM)

    lane_chunks = LANES // CHUNK
    pad_chunks = -(window_chunks + 1) % lane_chunks
    band = (window_chunks + pad_chunks + 1) * CHUNK
    front = band - CHUNK
    n_chunks = t_p // CHUNK
    need_p = (2 * (t_p * qw * 2 + 2 * t_p * hd * 4 + t_p * qw * 2) + 2 * (front + t_p) * hd * 2
              + n_chunks * group * CHUNK * band * 6 + (4 << 20))
    kv_w = kvh * hd
    o, kc_p, vc_p = pl.pallas_call(
        functools.partial(_swa_prompt_kernel, group=group, window_chunks=window_chunks, pad_chunks=pad_chunks,
                          unroll=2),
        out_shape=(jax.ShapeDtypeStruct((rows, att_w), BF16),
                   jax.ShapeDtypeStruct((b_p, window, kv_w), F32),
                   jax.ShapeDtypeStruct((b_p, window, kv_w), F32)),
        grid=(b_p, kvh),
        in_specs=[smem,
                  pl.BlockSpec((t_p, qw), lambda b, k: (b, k)),
                  pl.BlockSpec((t_p, hd), lambda b, k: (b, k_blk0 + k)),
                  pl.BlockSpec((t_p, hd), lambda b, k: (b, v_blk0 + k))],
        out_specs=(pl.BlockSpec((t_p, qw), lambda b, k: (b, k)),
                   pl.BlockSpec((None, window, hd), lambda b, k: (b, 0, k)),
                   pl.BlockSpec((None, window, hd), lambda b, k: (b, 0, k))),
        scratch_shapes=[pltpu.VMEM((front + t_p, hd), BF16), pltpu.VMEM((front + t_p, hd), BF16),
                        pltpu.VMEM((n_chunks, group * CHUNK, band), F32),
                        pltpu.VMEM((n_chunks, group * CHUNK, band), BF16)],
        compiler_params=_params(("parallel", "parallel"), need_p),
        name="swa_prompt",
    )(sinks, zb, z32, z32)

    assert t_s == CHUNK and window + t_s == (window_chunks + 1) * CHUNK
    rb0 = (b_p * t_p) // t_s
    assert cfg["off_k_a"] % kv_w == 0 and cfg["off_v_a"] % kv_w == 0
    o, kn_s, vn_s = pl.pallas_call(
        functools.partial(_swa_sample_kernel, group=group, hd=hd),
        out_shape=(jax.ShapeDtypeStruct((rows, att_w), BF16),
                   jax.ShapeDtypeStruct((b_s, t_s, kv_w), F32),
                   jax.ShapeDtypeStruct((b_s, t_s, kv_w), F32)),
        grid=(b_s,),
        in_specs=[smem,
                  pl.BlockSpec((t_s, att_w), lambda b: (rb0 + b, 0)),
                  pl.BlockSpec((t_s, kv_w), lambda b: (rb0 + b, cfg["off_k_a"] // kv_w)),
                  pl.BlockSpec((t_s, kv_w), lambda b: (rb0 + b, cfg["off_v_a"] // kv_w)),
                  pl.BlockSpec((None, None, window, kv_w), lambda b: (layer, b, 0, 0)),
                  pl.BlockSpec((None, None, window, kv_w), lambda b: (layer, b, 0, 0)),
                  pl.BlockSpec(memory_space=pl.ANY)],
        out_specs=(pl.BlockSpec((t_s, att_w), lambda b: (rb0 + b, 0)),
                   pl.BlockSpec((None, t_s, kv_w), lambda b: (b, 0, 0)),
                   pl.BlockSpec((None, t_s, kv_w), lambda b: (b, 0, 0))),
        input_output_aliases={6: 0},
        compiler_params=_params(("parallel",), 8 << 20),
        name="swa_sample",
    )(sinks, zb, z32, z32, cache_k, cache_v, o)
    return o, (kc_p, vc_p, kn_s, vn_s)


def _retention_state_update(k32, v_bf, zeta):
    kz_t = jnp.transpose(k32 * zeta).astype(BF16)
    return jnp.dot(kz_t, v_bf, preferred_element_type=F32)


def _retention_readout(q_bf, k_bf, v_bf, g32, state_bf, decay, xi, gain):
    s = lax.dot_general(q_bf, k_bf, (((1,), (1,)), ((), ())), preferred_element_type=F32) * decay
    o = (jnp.dot(s.astype(BF16), v_bf, preferred_element_type=F32)
         + jnp.dot(q_bf, state_bf, preferred_element_type=F32) * xi)
    mu = jnp.mean(o, axis=-1, keepdims=True)
    var = jnp.mean(jnp.square(o - mu), axis=-1, keepdims=True)
    on = (o - mu) * lax.rsqrt(var + GN_EPS) * gain
    return (g32 * jax.nn.sigmoid(g32)) * on


def _retention_prompt_kernel(cd_ref, q_ref, k_ref, v_ref, g_ref, decay_ref, xi_ref, zeta_ref, gain_ref,
                             o_ref, st_ref, sb_ref, *, block, unroll):
    h = pl.program_id(1)
    n_blocks = q_ref.shape[0] // block
    block_decay = cd_ref[h]
    zeta = zeta_ref[...]

    state = jnp.zeros(st_ref.shape, F32)
    for c in range(n_blocks):
        rows = slice(c * block, (c + 1) * block)
        sb_ref[c] = state.astype(BF16)
        state = block_decay * state + _retention_state_update(k_ref[rows, :], v_ref[rows, :].astype(BF16), zeta)
    st_ref[...] = state

    gain = gain_ref[pl.ds(h, 1), :]

    def body(t, carry):
        for u in range(unroll):
            c = t * unroll + u
            rows = pl.ds(pl.multiple_of(c * block, block), block)
            out = _retention_readout(q_ref[rows, :].astype(BF16), k_ref[rows, :].astype(BF16),
                                     v_ref[rows, :].astype(BF16), g_ref[rows, :].astype(F32), sb_ref[c],
                                     decay_ref[...], xi_ref[...], gain)
            o_ref[rows, :] = out.astype(o_ref.dtype)
        return carry

    lax.fori_loop(0, n_blocks // unroll, body, 0)


def _retention_sample_kernel(cd_ref, q_ref, k_ref, v_ref, g_ref, decay_ref, xi_ref, zeta_ref, gain_ref,
                             st_in_ref, o_in_ref, o_ref, st_ref):
    del o_in_ref
    n_heads, dk, dv = st_in_ref.shape
    head0 = pl.program_id(1) * n_heads
    for i in range(n_heads):
        qk_cols, v_cols = slice(i * dk, (i + 1) * dk), slice(i * dv, (i + 1) * dv)
        state = st_in_ref[i]
        k32 = k_ref[:, qk_cols]
        v_bf = v_ref[:, v_cols].astype(BF16)
        out = _retention_readout(q_ref[:, qk_cols].astype(BF16), k32.astype(BF16), v_bf, g_ref[:, v_cols].astype(F32),
                                 state.astype(BF16), decay_ref[i], xi_ref[i], gain_ref[pl.ds(head0 + i, 1), :])
        o_ref[:, v_cols] = out.astype(o_ref.dtype)
        st_ref[i] = cd_ref[head0 + i] * state + _retention_state_update(k32, v_bf, zeta_ref[i])


def retention(z32, zb, state_in, gain, layer, consts_p, consts_s, cfg):
    dk, dv, heads = cfg["ret_dk"], cfg["ret_dv"], cfg["ret_heads"]
    b_p, t_p, b_s, t_s = cfg["b_p"], cfg["t_p"], cfg["b_s"], cfg["t_s"]
    rows = z32.shape[0]
    smem = pl.BlockSpec(memory_space=pltpu.SMEM)
    col_offsets = (("off_q_r", dk), ("off_k_r", dk), ("off_v_rb", dv), ("off_g_rb", dv))

    def specs(t, length, hg, row_block0):
        lead, hg = hg, hg or 1
        col0 = [cfg[o] // (hg * w) for o, w in col_offsets]
        return ([pl.BlockSpec((t, hg * w), lambda b, h, c0=c0: (row_block0 + b, c0 + h))
                 for c0, (_, w) in zip(col0, col_offsets)] +
                [pl.BlockSpec((lead, length, length), lambda b, h: (h, 0, 0)),
                 pl.BlockSpec((lead, length, 1), lambda b, h: (h, 0, 0)),
                 pl.BlockSpec((lead, length, 1), lambda b, h: (h, 0, 0)),
                 pl.BlockSpec((heads, dv), lambda b, h: (0, 0))])

    decay, xi, zeta, block_decay = consts_p
    block = decay.shape[1]
    n_blocks = t_p // block
    unroll = max(u for u in (1, 2, 4, 8) if n_blocks % u == 0)
    need_p = (2 * (2 * t_p * dk * 4 + 2 * t_p * dv * 4 + t_p * dv * 2 + block * block * 4)
              + n_blocks * dk * dv * 2 + (8 << 20))
    rr, st_p = pl.pallas_call(
        functools.partial(_retention_prompt_kernel, block=block, unroll=unroll),
        out_shape=(jax.ShapeDtypeStruct((rows, heads * dv), BF16),
                   jax.ShapeDtypeStruct((b_p, heads, dk, dv), F32)),
        grid=(b_p, heads),
        in_specs=[smem] + specs(t_p, block, None, 0),
        out_specs=(pl.BlockSpec((t_p, dv), lambda b, h: (b, h)),
                   pl.BlockSpec((None, None, dk, dv), lambda b, h: (b, h, 0, 0))),
        scratch_shapes=[pltpu.VMEM((n_blocks, dk, dv), BF16)],
        compiler_params=_params(("parallel", "parallel"), need_p),
        name="retention_prompt",
    )(block_decay, zb, z32, zb, zb, decay, xi, zeta, gain)

    decay, xi, zeta, block_decay = consts_s
    assert decay.shape[1] == t_s
    rb0 = (b_p * t_p) // t_s
    hg = max(n for n in range(1, heads + 1)
             if heads % n == 0 and all(cfg[o] % (n * w) == 0 for o, w in col_offsets))
    rr, st_s = pl.pallas_call(
        _retention_sample_kernel,
        out_shape=(jax.ShapeDtypeStruct((rows, heads * dv), BF16),
                   jax.ShapeDtypeStruct((b_s, heads, dk, dv), F32)),
        grid=(b_s, heads // hg),
        in_specs=[smem] + specs(t_s, t_s, hg, rb0) +
                 [pl.BlockSpec((None, None, hg, dk, dv), lambda b, h: (layer, b, h, 0, 0)),
                  pl.BlockSpec(memory_space=pl.ANY)],
        out_specs=(pl.BlockSpec((t_s, hg * dv), lambda b, h: (rb0 + b, h)),
                   pl.BlockSpec((None, hg, dk, dv), lambda b, h: (b, h, 0, 0))),
        input_output_aliases={10: 0},
        compiler_params=_params(("parallel", "parallel"), 16 << 20),
        name="retention_sample",
    )(block_decay, zb, z32, zb, zb, decay, xi, zeta, gain, state_in, rr)
    return rr, st_p, st_s


def _merge_kernel(*refs, parts):
    (oa_ref, rr_ref, wa_ref, wr_ref, ga_ref, gr_ref), (o_ref,), cast_src, cast_dst, _ = _split_refs(refs, 6, 1)
    for rows in _row_parts(o_ref, parts):
        branch_a = jnp.dot(oa_ref[rows, :], wa_ref[...], preferred_element_type=F32)
        branch_r = jnp.dot(rr_ref[rows, :], wr_ref[...], preferred_element_type=F32)
        merged = (jax.nn.sigmoid(ga_ref[rows, :].astype(F32)) * branch_a
                  + jax.nn.sigmoid(gr_ref[rows, :].astype(F32)) * branch_r)
        o_ref[rows, :] = merged.astype(o_ref.dtype)
    _run_side_casts(cast_src, cast_dst)


def merge_branches(o_a, rr, w_a, w_r, zb, cfg, cast_jobs=()):
    rows, ka = o_a.shape
    kr = rr.shape[1]
    d = w_a.shape[1]
    tm = _pick(rows, 768, 8)
    tn = _pick(d, 1024, LANES)
    ni, nj = rows // tm, d // tn
    casts = [SideCast(st, layer, ni * nj, nj) for st, layer in cast_jobs]
    ga0, gr0 = cfg["off_gate_ab"] // tn, cfg["off_gate_rb"] // tn
    assert cfg["off_gate_ab"] % tn == 0 and cfg["off_gate_rb"] % tn == 0
    need = (2 * (tm * (ka + kr) * 2 + (ka + kr) * tn * 2 + 3 * tm * tn * 2) + 4 * (tm // 4) * tn * 4
            + sum(c.vmem for c in casts))
    out = pl.pallas_call(
        functools.partial(_merge_kernel, parts=4 if tm % 64 == 0 else 1),
        out_shape=[jax.ShapeDtypeStruct((rows, d), BF16)] + [c.out_shape for c in casts],
        grid=(ni, nj),
        in_specs=[pl.BlockSpec((tm, ka), lambda i, j: (i, 0)),
                  pl.BlockSpec((tm, kr), lambda i, j: (i, 0)),
                  pl.BlockSpec((ka, tn), lambda i, j: (0, j)),
                  pl.BlockSpec((kr, tn), lambda i, j: (0, j)),
                  pl.BlockSpec((tm, tn), lambda i, j: (i, ga0 + j)),
                  pl.BlockSpec((tm, tn), lambda i, j: (i, gr0 + j))] + [c.in_spec for c in casts],
        out_specs=[pl.BlockSpec((tm, tn), lambda i, j: (i, j))] + [c.out_spec for c in casts],
        compiler_params=_params(("arbitrary", "arbitrary"), need),
        name="merge_branches",
    )(o_a, rr, w_a, w_r, zb, zb, *[c.operand for c in casts])
    return out[0], out[1:]


def _matmul_residual_kernel(*refs, parts, fold_norm):
    ins, outs, cast_src, cast_dst, _ = _split_refs(refs, 4 if fold_norm else 3, 3 if fold_norm else 1)
    a_ref, w_ref, x_ref = ins[:3]
    o_ref = outs[0]
    if fold_norm:
        g_ref, (xg_ref, ssq_ref) = ins[3], outs[1:]

        @pl.when(pl.program_id(1) == 0)
        def _():
            ssq_ref[...] = jnp.zeros(ssq_ref.shape, F32)

    for rows in _row_parts(o_ref, parts):
        y = x_ref[rows, :] + jnp.dot(a_ref[rows, :], w_ref[...], preferred_element_type=F32)
        o_ref[rows, :] = y
        if fold_norm:
            xg_ref[rows, :] = (y * g_ref[...]).astype(BF16)
            ssq_ref[rows, :] += _lane_group_sum(y * y)
    _run_side_casts(cast_src, cast_dst)


def matmul_residual(a, w, x, tm_target, tn_target, next_gain=None, cast_jobs=()):
    rows, k = a.shape
    n = w.shape[1]
    tm = _pick(rows, tm_target, 8)
    tn = _pick(n, tn_target, LANES)
    ni, nj = rows // tm, n // tn
    fold_norm = next_gain is not None
    casts = [SideCast(st, layer, ni * nj, nj) for st, layer in cast_jobs]
    need = (2 * (tm * k * 2 + k * tn * 2 + 2 * tm * tn * 4 + tm * tn * 2 + tm * LANES * 4) + 3 * tm * tn * 4
            + sum(c.vmem for c in casts))
    tile = pl.BlockSpec((tm, tn), lambda i, j: (i, j))
    in_specs = [pl.BlockSpec((tm, k), lambda i, j: (i, 0)), pl.BlockSpec((k, tn), lambda i, j: (0, j)), tile]
    out_specs, out_shape, operands = [tile], [jax.ShapeDtypeStruct((rows, n), F32)], [a, w, x]
    if fold_norm:
        in_specs.append(pl.BlockSpec((1, tn), lambda i, j: (0, j)))
        operands.append(next_gain.reshape(1, n))
        out_specs += [tile, pl.BlockSpec((tm, LANES), lambda i, j: (i, 0))]
        out_shape += [jax.ShapeDtypeStruct((rows, n), BF16), jax.ShapeDtypeStruct((rows, LANES), F32)]
    out = pl.pallas_call(
        functools.partial(_matmul_residual_kernel, parts=4 if tm % 64 == 0 else 1, fold_norm=fold_norm),
        out_shape=out_shape + [c.out_shape for c in casts],
        grid=(ni, nj),
        in_specs=in_specs + [c.in_spec for c in casts],
        out_specs=out_specs + [c.out_spec for c in casts],
        compiler_params=_params(("arbitrary", "arbitrary"), need),
        name="matmul_residual",
    )(*operands, *[c.operand for c in casts])
    n_own = len(out_shape)
    return out[:n_own], out[n_own:]


def _gate_up_kernel(*refs, parts, d_model):
    (xg_ref, wg_ref, wu_ref, ssq_ref), (o_ref,), cast_src, cast_dst, (rinv_ref,) = _split_refs(refs, 4, 1, 1)
    _store_row_scale(ssq_ref, rinv_ref, d_model)
    for rows in _row_parts(o_ref, parts):
        xg = xg_ref[rows, :]
        rinv = rinv_ref[rows, :]
        gate_acc = jnp.dot(xg, wg_ref[...], preferred_element_type=F32)
        up_acc = jnp.dot(xg, wu_ref[...], preferred_element_type=F32)
        for g in range(o_ref.shape[1] // LANES):
            cols = slice(g * LANES, (g + 1) * LANES)
            gate = gate_acc[:, cols] * rinv
            up = up_acc[:, cols] * rinv
            o_ref[rows, cols] = ((gate * jax.nn.sigmoid(gate)) * up).astype(o_ref.dtype)
    _run_side_casts(cast_src, cast_dst)


def gate_up(xg, ssq, w_gate_up, cast_jobs=()):
    rows, d = xg.shape
    f = w_gate_up.shape[1] // 2
    tm = _pick(rows, 1536, 8)
    tn = _pick(f, 256, LANES)
    ni, nj = rows // tm, f // tn
    casts = [SideCast(st, layer, ni * nj, nj) for st, layer in cast_jobs]
    need = (2 * (tm * d * 2 + 2 * d * tn * 2 + tm * tn * 2 + tm * LANES * 4) + 4 * tm * tn * 4 + tm * LANES * 4
            + sum(c.vmem for c in casts))
    out = pl.pallas_call(
        functools.partial(_gate_up_kernel, parts=2 if tm % 16 == 0 else 1, d_model=d),
        out_shape=[jax.ShapeDtypeStruct((rows, f), BF16)] + [c.out_shape for c in casts],
        grid=(ni, nj),
        in_specs=[pl.BlockSpec((tm, d), lambda i, j: (i, 0)),
                  pl.BlockSpec((d, tn), lambda i, j: (0, j)),
                  pl.BlockSpec((d, tn), lambda i, j: (0, nj + j)),
                  pl.BlockSpec((tm, LANES), lambda i, j: (i, 0))] + [c.in_spec for c in casts],
        out_specs=[pl.BlockSpec((tm, tn), lambda i, j: (i, j))] + [c.out_spec for c in casts],
        scratch_shapes=[pltpu.VMEM((tm, LANES), F32)],
        compiler_params=_params(("arbitrary", "arbitrary"), need),
        name="swiglu_gate_up",
    )(xg, w_gate_up, w_gate_up, ssq, *[c.operand for c in casts])
    return out[0], out[1:]


def _rope_tables(positions, hd):
    half = hd // 2
    inv = ROPE_THETA ** (-jnp.arange(half, dtype=F32) / half)
    ang = positions.astype(F32)[:, None] * inv[None, :]
    cos, sin = jnp.cos(ang), jnp.sin(ang)
    return jnp.concatenate([cos, cos, -sin, sin], axis=-1)


def _retention_tables(heads, length):
    lg = jnp.log1p(-jnp.exp2(-5.0 - jnp.arange(heads, dtype=F32)))
    j = jnp.arange(length, dtype=F32)
    rel = j[:, None] - j[None, :]
    decay = jnp.where(rel >= 0, jnp.exp(jnp.maximum(rel, 0.0)[None] * lg[:, None, None]), 0.0)
    xi = jnp.exp((j + 1.0)[None, :] * lg[:, None])[:, :, None]
    zeta = jnp.exp((length - 1.0 - j)[None, :] * lg[:, None])[:, :, None]
    block_decay = jnp.exp(length * lg)
    return decay, xi, zeta, block_decay


def kernel(x_prompt, x_sample, cache_swa_k, cache_swa_v, state_retention, w_in, w_proj_a, w_proj_r, w_out,
           attn_sinks, ret_norm_gain, norm_mix, norm_ffn, w_gate_up, w_down, norm_final):
    b_p, t_p, d = x_prompt.shape
    b_s, t_s, _ = x_sample.shape
    depth = w_in.shape[0]
    _, _, window, kvh, hd = cache_swa_k.shape
    att_heads = attn_sinks.shape[1]
    _, _, ret_heads, ret_dk, ret_dv = state_retention.shape
    att_w, kv_w = att_heads * hd, kvh * hd
    rqk_w, rv_w = ret_heads * ret_dk, ret_heads * ret_dv
    splits = (att_w, kv_w, kv_w, rqk_w, rqk_w, rv_w, rv_w, d, d)
    offs = [0]
    for s in splits:
        offs.append(offs[-1] + s)
    assert offs[-1] == w_in.shape[2] and hd == LANES and ret_dk == LANES
    split_kind = (OP_ROPE_SCALED + DST_BF16, OP_ROPE, OP_PLAIN, OP_ROPE + DST_BF16, OP_ROPE_SCALED,
                  OP_PLAIN + DST_BF16, OP_PLAIN + DST_BF16, OP_PLAIN + DST_BF16, OP_PLAIN + DST_BF16)
    assert hd == ret_dk
    tn_in = _pick(kv_w, 512, LANES)
    assert all(o % tn_in == 0 for o in offs)
    tile_kinds = [kd for kd, s in zip(split_kind, splits) for _ in range(s // tn_in)]
    off_in, fill = [], {False: 0, True: 0}
    for kd, s in zip(split_kind, splits):
        off_in.append(fill[kd >= DST_BF16])
        fill[kd >= DST_BF16] += s
    cfg = dict(b_p=b_p, t_p=t_p, b_s=b_s, t_s=t_s, head_dim=hd, kv_heads=kvh, att_group=att_heads // kvh,
               att_width=att_w, ret_heads=ret_heads, ret_dk=ret_dk, ret_dv=ret_dv,
               off_q_a=off_in[0], off_k_a=off_in[1], off_v_a=off_in[2], off_q_r=off_in[3], off_k_r=off_in[4],
               off_v_rb=off_in[5], off_g_rb=off_in[6], off_gate_ab=off_in[7], off_gate_rb=off_in[8])

    pos = jnp.concatenate([jnp.tile(jnp.arange(t_p, dtype=jnp.int32), b_p),
                           jnp.tile(PAST_LEN + jnp.arange(t_s, dtype=jnp.int32), b_s)])
    rope_t = _rope_tables(pos, hd)
    ret_consts_p = _retention_tables(ret_heads, _pick(t_p, RET_BLOCK_TARGET, CHUNK))
    ret_consts_s = _retention_tables(ret_heads, t_s)

    cache_k = cache_swa_k.reshape(depth, b_s, window, kv_w)
    cache_v = cache_swa_v.reshape(depth, b_s, window, kv_w)
    w_in_b, w_a_b, w_r_b, w_out_b, w_down_b = (w[0].astype(BF16) for w in (w_in, w_proj_a, w_proj_r, w_out, w_down))

    rows_p = b_p * t_p
    x, xg, ssq = norm_prep(x_prompt.reshape(rows_p, d), x_sample.reshape(b_s * t_s, d), norm_mix[0])

    kp_rows, vp_rows, sp_states, ks_rows, vs_rows, ss_states = [], [], [], [], [], []
    for layer in range(depth):
        nxt = layer + 1

        def next_layer(*stacked):
            return [(w, nxt) for w in stacked] if nxt < depth else []

        z32, zb, (w_gu_b,) = inproj(xg, ssq, w_in_b, tile_kinds, rope_t, tn_in, hd ** -0.5, [(w_gate_up, layer)])
        o_a, (kc_p, vc_p, kn_s, vn_s) = swa(z32, zb, attn_sinks[layer], cache_k, cache_v, layer, cfg)
        rr, st_p, st_s = retention(z32, zb, state_retention, ret_norm_gain[layer], layer,
                                   ret_consts_p, ret_consts_s, cfg)
        merged, _ = merge_branches(o_a, rr, w_a_b, w_r_b, zb, cfg)
        (x, xg, ssq), next_proj = matmul_residual(merged, w_out_b, x, 768, 1024, norm_ffn[layer],
                                                  next_layer(w_out, w_proj_a, w_proj_r))
        act, next_in = gate_up(xg, ssq, w_gu_b, next_layer(w_in))
        if nxt < depth:
            (x, xg, ssq), next_down = matmul_residual(act, w_down_b, x, 768, 256, norm_mix[nxt], next_layer(w_down))
            (w_out_b, w_a_b, w_r_b), (w_in_b,), (w_down_b,) = next_proj, next_in, next_down
        else:
            (x,), _ = matmul_residual(act, w_down_b, x, 768, 256)

        kp_rows.append(kc_p.reshape(b_p, window, kvh, hd))
        vp_rows.append(vc_p.reshape(b_p, window, kvh, hd))
        ks_rows.append(kn_s.reshape(b_s, t_s, kvh, hd))
        vs_rows.append(vn_s.reshape(b_s, t_s, kvh, hd))
        sp_states.append(st_p)
        ss_states.append(st_s)

    y_p, y_s = rmsnorm_split(x, norm_final, rows_p)
    return (y_p.reshape(b_p, t_p, d), y_s.reshape(b_s, t_s, d),
            jnp.stack(kp_rows), jnp.stack(vp_rows), jnp.stack(sp_states),
            jnp.stack(ks_rows), jnp.stack(vs_rows), jnp.stack(ss_states))
```
